```python
import math
import jax
import jax.numpy as jnp
from jax import lax
import numpy as np

D_MODEL = 1024
BATCH = 4
SEQ = 4096
DEPTH = 4
DEC_BATCH = 32
DEC_SEQ = 4
PAST_LEN = 8192
PAGE_SIZE = 128

HEAD_DIM = 64
DA_HEADS = 4
DA_VDIM = 2 * HEAD_DIM
DA_QK = DA_HEADS * 2 * HEAD_DIM
DA_V = DA_HEADS * DA_VDIM
NSA_HEADS = 8
NSA_GROUPS = 2
NSA_REP = NSA_HEADS // NSA_GROUPS
NSA_Q = NSA_HEADS * HEAD_DIM
NSA_KV_W = NSA_GROUPS * HEAD_DIM
NSA_BLOCK = 64
NSA_TOPK = 16
NSA_WINDOW = 512
NSA_CMP_HID = 2 * HEAD_DIM
N_BRANCH = 3
HG_HEADS = 8
HG_DK = D_MODEL // HG_HEADS
HG_DV = D_MODEL // HG_HEADS
HG_CHUNK = 64
D_FF = ((8 * D_MODEL // 3 + 255) // 256) * 256

N_EVEN = (DEPTH + 1) // 2
N_ODD = DEPTH // 2
EVEN_SIZES = (DA_QK, DA_QK, DA_V, NSA_Q, 6 * NSA_KV_W, NSA_HEADS * N_BRANCH)
EVEN_IN = sum(EVEN_SIZES)
D_MIX_EVEN = DA_V + NSA_Q
ODD_SIZES = (HG_HEADS * HG_DK, HG_HEADS * HG_DK, HG_HEADS * HG_DV, HG_HEADS * HG_DV)
ODD_IN = sum(ODD_SIZES)
ATTN_QB = 128
SEL_QB = 64
WIN_QB = 128
NEG = -1e30
EPS = 1e-6

kernel_name = 'hybrid_diffattn_nsa_hgrn2_step'


def split_at(x, sizes):
    return jnp.split(x, np.cumsum(sizes)[:-1].tolist(), axis=-1)


def rms_norm(x, g):
    xf = x.astype(jnp.float32)
    y = xf * lax.rsqrt(jnp.mean(xf * xf, axis=-1, keepdims=True) + EPS)
    return (y * g.astype(jnp.float32)).astype(x.dtype)


def masked_softmax(s, mask):
    s = jnp.where(mask, s.astype(jnp.float32), NEG)
    m = jnp.max(s, axis=-1, keepdims=True)
    e = jnp.where(mask, jnp.exp(s - m), 0.0)
    return e / jnp.maximum(jnp.sum(e, axis=-1, keepdims=True), 1e-30)


def query_blocks(x, qb):
    b, t = x.shape[0], x.shape[1]
    return jnp.moveaxis(x.reshape((b, t // qb, qb) + x.shape[2:]), 1, 0)


def merge_blocks(y):
    y = jnp.moveaxis(y, 0, 1)
    return y.reshape((y.shape[0], y.shape[1] * y.shape[2]) + y.shape[3:])


def swiglu(h, w_in, w_out):
    gate, up = jnp.split(h @ w_in, 2, axis=-1)
    return (jax.nn.silu(gate) * up) @ w_out


def diff_attention(q, k, v, q_pos, k_pos, lam):
    scale = HEAD_DIM ** -0.5
    qb = math.gcd(q.shape[1], ATTN_QB)

    def block(args):
        qblk, pblk = args
        s = jnp.einsum('bqhmd,bkhmd->bmhqk', qblk, k) * scale
        p = masked_softmax(s, k_pos[None, :] <= pblk[:, None])
        a = (p[:, 0] - lam * p[:, 1]).astype(v.dtype)
        return jnp.einsum('bhqk,bkhd->bqhd', a, v)

    return merge_blocks(lax.map(block, (query_blocks(q, qb), q_pos.reshape(-1, qb))))


def nsa_compress(rows, pe, w1, w2):
    b, tk = rows.shape[0], rows.shape[1]
    n_blk = -(-tk // NSA_BLOCK)
    rows = jnp.pad(rows, ((0, 0), (0, n_blk * NSA_BLOCK - tk), (0, 0), (0, 0)))
    blocks = rows.reshape(b, n_blk, NSA_BLOCK, NSA_GROUPS, HEAD_DIM) + pe[:, None, :]
    hid = jax.nn.gelu(jnp.einsum('bnlgd,lde->bnge', blocks, w1))
    return jnp.einsum('bnge,ed->bngd', hid, w2)


def nsa_cmp_sel(q, kc, vc, ks, vs, q_pos, pe, w1, w2, kc_gain):
    b, tk = ks.shape[0], ks.shape[1]
    scale = HEAD_DIM ** -0.5
    kcmp = rms_norm(nsa_compress(kc, pe[0], w1[0], w2[0]), kc_gain)
    vcmp = nsa_compress(vc, pe[1], w1[1], w2[1])
    n_blk = kcmp.shape[1]
    blk = jnp.arange(n_blk)
    s = jnp.einsum('bqgrd,bngd->bqgrn', q, kcmp) * scale
    cmp_ok = blk[None, :] < ((q_pos + 1) // NSA_BLOCK)[:, None]
    p = masked_softmax(s, cmp_ok[None, :, None, None, :])
    o_cmp = jnp.einsum('bqgrn,bngd->bqgrd', p.astype(vcmp.dtype), vcmp)
    cur = q_pos // NSA_BLOCK
    cand = blk[None, :] < cur[:, None]
    imp = jnp.where(cand[None, :, None, :], jnp.sum(p, axis=3), -1.0)
    _, top = lax.top_k(imp, min(NSA_TOPK - 1, n_blk))
    cur_b = jnp.broadcast_to(cur[None, :, None, None], top.shape[:3] + (1,))
    idx = jnp.concatenate([top, cur_b], axis=-1)
    ok = jnp.concatenate([top < cur[None, :, None, None], jnp.ones(cur_b.shape, bool)], axis=-1)
    pad = n_blk * NSA_BLOCK - tk

    def to_blocks(x):
        x = jnp.pad(x, ((0, 0), (0, pad), (0, 0), (0, 0)))
        return jnp.moveaxis(x.reshape(b, n_blk, NSA_BLOCK, NSA_GROUPS, HEAD_DIM), 3, 1)

    ksb, vsb = to_blocks(ks), to_blocks(vs)
    bi = jnp.arange(b)[:, None, None, None]
    gi = jnp.arange(NSA_GROUPS)[None, None, :, None]
    offs = jnp.arange(NSA_BLOCK)

    def sel_block(args):
        qblk, iblk, okblk, pblk = args
        kg = ksb[bi, gi, iblk]
        vg = vsb[bi, gi, iblk]
        kpos = iblk[..., None] * NSA_BLOCK + offs
        m = okblk[..., None] & (kpos <= pblk[None, :, None, None, None])
        sc = jnp.einsum('bqgrd,bqgkld->bqgrkl', qblk, kg) * scale
        shp = sc.shape
        pr = masked_softmax(sc.reshape(shp[:4] + (-1,)), m.reshape(m.shape[:3] + (1, -1))).reshape(shp)
        return jnp.einsum('bqgrkl,bqgkld->bqgrd', pr.astype(vg.dtype), vg)

    qb = math.gcd(q.shape[1], SEL_QB)
    o_sel = merge_blocks(lax.map(sel_block, (query_blocks(q, qb), query_blocks(idx, qb),
                                             query_blocks(ok, qb), q_pos.reshape(-1, qb))))
    return o_cmp, o_sel


def window_attend(q, k, v, q_pos, k_pos):
    s = jnp.einsum('...qgrd,...kgd->...qgrk', q, k) * (HEAD_DIM ** -0.5)
    km = k_pos[..., None, :]
    qm = q_pos[..., :, None]
    mask = (km <= qm) & (km >= qm - NSA_WINDOW) & (km >= 0)
    p = masked_softmax(s, mask[..., :, None, None, :])
    return jnp.einsum('...qgrk,...kgd->...qgrd', p.astype(v.dtype), v)


def window_prompt(q, kw, vw):
    b, t = q.shape[0], q.shape[1]
    nb = t // WIN_QB
    n_band = NSA_WINDOW // WIN_QB + 1
    padw = ((0, 0), (NSA_WINDOW, 0), (0, 0), (0, 0))
    kp = jnp.pad(kw, padw).reshape(b, nb + n_band - 1, WIN_QB, NSA_GROUPS, HEAD_DIM)
    vp = jnp.pad(vw, padw).reshape(b, nb + n_band - 1, WIN_QB, NSA_GROUPS, HEAD_DIM)
    kband = jnp.concatenate([kp[:, j:j + nb] for j in range(n_band)], axis=2)
    vband = jnp.concatenate([vp[:, j:j + nb] for j in range(n_band)], axis=2)
    q_pos = jnp.arange(t).reshape(nb, WIN_QB)
    k_pos = jnp.arange(nb)[:, None] * WIN_QB - NSA_WINDOW + jnp.arange(n_band * WIN_QB)[None, :]
    out = lax.map(lambda a: window_attend(a[0], a[1], a[2], a[3], a[4]),
                  (query_blocks(q, WIN_QB), jnp.moveaxis(kband, 1, 0), jnp.moveaxis(vband, 1, 0), q_pos, k_pos))
    return merge_blocks(out)


def even_mixer(h, q_pos, past_diff, past_nsa, past_win, w_in, w_out, qk_gain_a, lam_par, subln_gain,
               qk_gain_b, cmp_pe, cmp_w1, cmp_w2, lam_init):
    b, t = h.shape[0], h.shape[1]
    dq, dk, dv, nq, nkv, ng = split_at(h @ w_in, EVEN_SIZES)
    dq = rms_norm(dq.reshape(b, t, DA_HEADS, 2, HEAD_DIM), qk_gain_a[0])
    dk = rms_norm(dk.reshape(b, t, DA_HEADS, 2, HEAD_DIM), qk_gain_a[1])
    dv = dv.reshape(b, t, DA_HEADS, DA_VDIM)
    diff_rows = jnp.stack([dk.reshape(b, t, DA_HEADS, DA_VDIM), dv], axis=2)
    nq = rms_norm(nq.reshape(b, t, NSA_GROUPS, NSA_REP, HEAD_DIM), qk_gain_b[0])
    nkv = nkv.reshape(b, t, 6, NSA_GROUPS, HEAD_DIM)
    k_sel = rms_norm(nkv[:, :, 2], qk_gain_b[2])
    k_win = rms_norm(nkv[:, :, 4], qk_gain_b[3])
    nsa_rows = jnp.stack([nkv[:, :, 0], nkv[:, :, 1], k_sel, nkv[:, :, 3]], axis=2)
    win_rows = jnp.stack([k_win, nkv[:, :, 5]], axis=2)
    gates = jax.nn.sigmoid(ng.astype(jnp.float32)).reshape(b, t, NSA_GROUPS, NSA_REP, N_BRANCH)
    if past_diff is None:
        diff_all, nsa_all = diff_rows, nsa_rows
    else:
        diff_all = jnp.concatenate([past_diff, diff_rows], axis=1)
        nsa_all = jnp.concatenate([past_nsa, nsa_rows], axis=1)
    tk = diff_all.shape[1]
    k_pos = jnp.arange(tk)
    lp = lam_par.astype(jnp.float32)
    lam = jnp.exp(jnp.sum(lp[0] * lp[1])) - jnp.exp(jnp.sum(lp[2] * lp[3])) + lam_init
    da = diff_attention(dq, diff_all[:, :, 0].reshape(b, tk, DA_HEADS, 2, HEAD_DIM), diff_all[:, :, 1],
                        q_pos, k_pos, lam)
    da = (rms_norm(da, subln_gain) * (1.0 - lam_init)).reshape(b, t, DA_V)
    o_cmp, o_sel = nsa_cmp_sel(nq, nsa_all[:, :, 0], nsa_all[:, :, 1], nsa_all[:, :, 2], nsa_all[:, :, 3],
                               q_pos, cmp_pe, cmp_w1, cmp_w2, qk_gain_b[1])
    if past_win is None:
        o_win = window_prompt(nq, win_rows[:, :, 0], win_rows[:, :, 1])
        new_win = win_rows[:, -min(NSA_WINDOW, t):]
    else:
        nbuf = past_win.shape[1]
        win_all = jnp.concatenate([past_win, win_rows], axis=1)
        kpw = PAST_LEN - nbuf + jnp.arange(nbuf + t)
        o_win = window_attend(nq, win_all[:, :, 0], win_all[:, :, 1], q_pos, kpw)
        new_win = win_all[:, -nbuf:]
    o_nsa = (gates[..., 0:1] * o_cmp + gates[..., 1:2] * o_sel + gates[..., 2:3] * o_win)
    o_nsa = o_nsa.astype(h.dtype).reshape(b, t, NSA_Q)
    out = jnp.concatenate([da, o_nsa], axis=-1) @ w_out
    return out, diff_rows, nsa_rows, new_win


def gla_chunk(q, k, v, logf, s0):
    b, t = q.shape[0], q.shape[1]
    c = math.gcd(t, HG_CHUNK)
    causal = jnp.tril(jnp.ones((c, c), bool))[None, :, :, None, None]

    def chunks(x):
        return jnp.moveaxis(x.reshape((b, t // c, c) + x.shape[2:]), 1, 0)

    def step(s, xs):
        qc, kc, vc, gc = xs
        cum = jnp.cumsum(gc, axis=1)
        inter = jnp.einsum('bthk,bhkv->bthv', qc * jnp.exp(cum), s)
        dec = jnp.exp(jnp.where(causal, cum[:, :, None] - cum[:, None, :], -jnp.inf))
        att = jnp.einsum('bthk,btshk,bshk->btsh', qc, dec, kc)
        intra = jnp.einsum('btsh,bshv->bthv', att, vc)
        last = cum[:, -1]
        s_new = jnp.exp(last)[..., None] * s + jnp.einsum('bshk,bshv->bhkv', kc * jnp.exp(last[:, None] - cum), vc)
        return s_new, inter + intra

    s_fin, o = lax.scan(step, s0.astype(jnp.float32), (chunks(q), chunks(k), chunks(v), chunks(logf)))
    return merge_blocks(o), s_fin


def hgrn_mixer(h, s0, w_in, w_out, norm_gain, lb):
    b, t = h.shape[0], h.shape[1]
    q, f, i, g = split_at(h @ w_in, ODD_SIZES)
    q = (jax.nn.silu(q.astype(jnp.float32)) * HG_DK ** -0.5).reshape(b, t, HG_HEADS, HG_DK)
    fg = lb + (1.0 - lb) * jax.nn.sigmoid(f.astype(jnp.float32))
    logf = jnp.log(fg).reshape(b, t, HG_HEADS, HG_DK)
    k = (1.0 - fg).reshape(b, t, HG_HEADS, HG_DK)
    v = i.astype(jnp.float32).reshape(b, t, HG_HEADS, HG_DV)
    o, s_fin = gla_chunk(q, k, v, logf, s0)
    o = rms_norm(o, norm_gain.reshape(HG_HEADS, HG_DV)) * jax.nn.silu(g.astype(jnp.float32)).reshape(b, t, HG_HEADS, HG_DV)
    return o.reshape(b, t, HG_HEADS * HG_DV).astype(h.dtype) @ w_out, s_fin


def setup_inputs(seed: int = 0) -> dict:
    key = jax.random.key(seed)
    keys = iter(jax.random.split(key, 32))
    f32 = jnp.float32

    def normal(shape, scale):
        return jax.random.normal(next(keys), shape, f32) * scale

    def gain(shape):
        return 1.0 + normal(shape, 0.02)

    n_pages = PAST_LEN // PAGE_SIZE
    n_pool = (5 * DEC_BATCH * n_pages) // 4
    win_buf = min(NSA_WINDOW, PAST_LEN)
    x_prompt = normal((BATCH, SEQ, D_MODEL), 1.0)
    x_sample = normal((DEC_BATCH, DEC_SEQ, D_MODEL), 1.0)
    cache_diff_kv = normal((N_EVEN, n_pool, PAGE_SIZE, 2, DA_HEADS, DA_VDIM), 1.0)
    cache_nsa_kv = normal((N_EVEN, n_pool, PAGE_SIZE, 4, NSA_GROUPS, HEAD_DIM), 1.0)
    cache_nsa_win = normal((N_EVEN, DEC_BATCH, win_buf, 2, NSA_GROUPS, HEAD_DIM), 1.0)
    state_hgrn = normal((N_ODD, DEC_BATCH, HG_HEADS, HG_DK, HG_DV), 0.5)
    page_table = jax.random.permutation(next(keys), n_pool)[:DEC_BATCH * n_pages]
    page_table = page_table.reshape(DEC_BATCH, n_pages).astype(jnp.int32)
    return {
        'x_prompt': x_prompt,
        'x_sample': x_sample,
        'cache_diff_kv': cache_diff_kv,
        'cache_nsa_kv': cache_nsa_kv,
        'cache_nsa_win': cache_nsa_win,
        'state_hgrn': state_hgrn,
        'page_table': page_table,
        'norm_mix': gain((DEPTH, D_MODEL)),
        'norm_ffn': gain((DEPTH, D_MODEL)),
        'w_ffn_in': normal((DEPTH, D_MODEL, 2 * D_FF), D_MODEL ** -0.5),
        'w_ffn_out': normal((DEPTH, D_FF, D_MODEL), D_FF ** -0.5),
        'w_in_even': normal((N_EVEN, D_MODEL, EVEN_IN), D_MODEL ** -0.5),
        'w_out_even': normal((N_EVEN, D_MIX_EVEN, D_MODEL), D_MIX_EVEN ** -0.5),
        'diff_qk_gain': gain((N_EVEN, 2, HEAD_DIM)),
        'diff_lambda': normal((N_EVEN, 4, HEAD_DIM), 0.1),
        'diff_subln_gain': gain((N_EVEN, DA_VDIM)),
        'nsa_qk_gain': gain((N_EVEN, 4, HEAD_DIM)),
        'nsa_cmp_pe': normal((N_EVEN, 2, NSA_BLOCK, HEAD_DIM), 0.1),
        'nsa_cmp_w1': normal((N_EVEN, 2, NSA_BLOCK, HEAD_DIM, NSA_CMP_HID), (NSA_BLOCK * HEAD_DIM) ** -0.5),
        'nsa_cmp_w2': normal((N_EVEN, 2, NSA_CMP_HID, HEAD_DIM), NSA_CMP_HID ** -0.5),
        'w_in_odd': normal((N_ODD, D_MODEL, ODD_IN), D_MODEL ** -0.5),
        'w_out_odd': normal((N_ODD, HG_HEADS * HG_DV, D_MODEL), (HG_HEADS * HG_DV) ** -0.5),
        'hgrn_norm_gain': gain((N_ODD, HG_HEADS * HG_DV)),
        'hgrn_lb_logits': normal((DEPTH, HG_HEADS * HG_DK), 0.1),
    }


def reference(x_prompt, x_sample, cache_diff_kv, cache_nsa_kv, cache_nsa_win, state_hgrn, page_table,
              norm_mix, norm_ffn, w_ffn_in, w_ffn_out, w_in_even, w_out_even, diff_qk_gain, diff_lambda,
              diff_subln_gain, nsa_qk_gain, nsa_cmp_pe, nsa_cmp_w1, nsa_cmp_w2, w_in_odd, w_out_odd,
              hgrn_norm_gain, hgrn_lb_logits):
    f32 = jnp.float32
    lbw = jax.nn.softmax(hgrn_lb_logits.astype(f32), axis=0)
    lower_bounds = jnp.cumsum(lbw, axis=0) - lbw[0]
    pos_p = jnp.arange(x_prompt.shape[1])
    pos_s = PAST_LEN + jnp.arange(x_sample.shape[1])
    n_seq_s = page_table.shape[0]
    hp, hs = x_prompt, x_sample
    dkv_p, dkv_s, nkv_p, nkv_s, win_p, win_s, hg_p, hg_s = [], [], [], [], [], [], [], []
    for l in range(DEPTH):
        if l % 2 == 0:
            e = l // 2
            lam_init = 0.8 - 0.6 * math.exp(-0.3 * l)
            wts = (w_in_even[e], w_out_even[e], diff_qk_gain[e], diff_lambda[e], diff_subln_gain[e],
                   nsa_qk_gain[e], nsa_cmp_pe[e], nsa_cmp_w1[e], nsa_cmp_w2[e], lam_init)
            past_diff = cache_diff_kv[e][page_table].reshape(n_seq_s, -1, 2, DA_HEADS, DA_VDIM)
            past_nsa = cache_nsa_kv[e][page_table].reshape(n_seq_s, -1, 4, NSA_GROUPS, HEAD_DIM)
            mp, rdp, rnp, wp = even_mixer(rms_norm(hp, norm_mix[l]), pos_p, None, None, None, *wts)
            ms, rds, rns, wsb = even_mixer(rms_norm(hs, norm_mix[l]), pos_s, past_diff, past_nsa,
                                           cache_nsa_win[e], *wts)
            dkv_p.append(rdp)
            dkv_s.append(rds)
            nkv_p.append(rnp)
            nkv_s.append(rns)
            win_p.append(wp)
            win_s.append(wsb)
        else:
            r = l // 2
            s0 = jnp.zeros((hp.shape[0], HG_HEADS, HG_DK, HG_DV), f32)
            mp, sp = hgrn_mixer(rms_norm(hp, norm_mix[l]), s0, w_in_odd[r], w_out_odd[r],
                                hgrn_norm_gain[r], lower_bounds[l])
            ms, ss = hgrn_mixer(rms_norm(hs, norm_mix[l]), state_hgrn[r], w_in_odd[r], w_out_odd[r],
                                hgrn_norm_gain[r], lower_bounds[l])
            hg_p.append(sp)
            hg_s.append(ss)
        hp = hp + mp
        hs = hs + ms
        hp = hp + swiglu(rms_norm(hp, norm_ffn[l]), w_ffn_in[l], w_ffn_out[l])
        hs = hs + swiglu(rms_norm(hs, norm_ffn[l]), w_ffn_in[l], w_ffn_out[l])
    return (hp, hs, jnp.stack(dkv_p), jnp.stack(dkv_s), jnp.stack(nkv_p), jnp.stack(nkv_s),
            jnp.stack(win_p), jnp.stack(win_s), jnp.stack(hg_p), jnp.stack(hg_s))
```

```python
import functools
import math

import jax
import jax.numpy as jnp
from jax import lax
from jax.experimental import pallas as pl
from jax.experimental.pallas import tpu as pltpu

F32 = jnp.float32
BF16 = jnp.bfloat16

D_MODEL = 1024
HEAD_DIM = 64
DA_HEADS = 4
DA_VDIM = 2 * HEAD_DIM
NSA_HEADS = 8
NSA_GROUPS = 2
NSA_REP = NSA_HEADS // NSA_GROUPS
NSA_BLOCK = 64
NSA_TOPK = 16
NSA_WINDOW = 512
NSA_CMP_HID = 2 * HEAD_DIM
HG_HEADS = 8
HG_DK = D_MODEL // HG_HEADS
HG_CHUNK = 64
EPS = 1e-6
NEG = -1e30

LANES = 128
SUBLANES = 8
VMEM_LIMIT = 56 * 1024 * 1024

C_DQ, C_DK, C_DV, C_NQ, C_NKV, C_NG = 0, 512, 1024, 1536, 2048, 2816
EVEN_IN = 2840
EVEN_IN_PAD = 2944
EVEN_NORM_BLOCKS = tuple(range(0, 8)) + tuple(range(12, 16)) + (18, 20)
ODD_IN = 4096
SAMPLE_ROWS = 16


def _cparams(sem):
    return pltpu.CompilerParams(dimension_semantics=sem, vmem_limit_bytes=VMEM_LIMIT)


def _nt_dot(a, b):
    return lax.dot_general(a, b, (((1,), (1,)), ((), ())), preferred_element_type=F32)


def _rms(x, gain):
    return x * lax.rsqrt(jnp.mean(x * x, axis=-1, keepdims=True) + EPS) * gain


def _seg_rms(y, gain):
    lo = lax.broadcasted_iota(jnp.int32, y.shape, 1) < HEAD_DIM
    y2 = y * y
    s_lo = jnp.sum(jnp.where(lo, y2, 0.0), axis=-1, keepdims=True)
    s_hi = jnp.sum(jnp.where(lo, 0.0, y2), axis=-1, keepdims=True)
    ms = jnp.where(lo, s_lo, s_hi) * (1.0 / HEAD_DIM)
    return y * lax.rsqrt(ms + EPS) * gain


def _norm_proj_kernel(x_ref, g_ref, w_ref, cg_ref, o_ref, ob_ref, *, norm_blocks, col_chunk):
    xn = _rms(x_ref[...], g_ref[...]).astype(BF16)
    n_cols = w_ref.shape[1]
    for c0 in range(0, n_cols, col_chunk):
        c1 = min(c0 + col_chunk, n_cols)
        y = jnp.dot(xn, w_ref[:, c0:c1], preferred_element_type=F32)
        for b in range(c0 // LANES, c1 // LANES):
            yb = y[:, b * LANES - c0:(b + 1) * LANES - c0]
            if b in norm_blocks:
                yb = _seg_rms(yb, cg_ref[:, b * LANES:(b + 1) * LANES])
            o_ref[:, b * LANES:(b + 1) * LANES] = yb
            ob_ref[:, b * LANES:(b + 1) * LANES] = yb.astype(BF16)


def _norm_proj(x2d, gain, w_bf16, colgain, norm_blocks, tm):
    n, d = x2d.shape
    c = w_bf16.shape[1]
    kern = functools.partial(_norm_proj_kernel, norm_blocks=norm_blocks, col_chunk=512)
    return pl.pallas_call(
        kern,
        grid=(n // tm,),
        in_specs=[
            pl.BlockSpec((tm, d), lambda i: (i, 0)),
            pl.BlockSpec((1, d), lambda i: (0, 0)),
            pl.BlockSpec((d, c), lambda i: (0, 0)),
            pl.BlockSpec((1, c), lambda i: (0, 0)),
        ],
        out_specs=[
            pl.BlockSpec((tm, c), lambda i: (i, 0)),
            pl.BlockSpec((tm, c), lambda i: (i, 0)),
        ],
        out_shape=[jax.ShapeDtypeStruct((n, c), F32), jax.ShapeDtypeStruct((n, c), BF16)],
        compiler_params=_cparams(("parallel",)),
    )(x2d, gain.reshape(1, d), w_bf16, colgain)


def _ffn_kernel(x_ref, g_ref, wg_ref, wu_ref, wo_ref, o_ref, xn_scr, acc_scr):
    f = pl.program_id(1)

    @pl.when(f == 0)
    def _():
        xn_scr[...] = _rms(x_ref[...], g_ref[...]).astype(BF16)
        acc_scr[...] = jnp.zeros_like(acc_scr)

    xn = xn_scr[...]
    gate = jnp.dot(xn, wg_ref[...], preferred_element_type=F32)
    up = jnp.dot(xn, wu_ref[...], preferred_element_type=F32)
    act = (gate * jax.nn.sigmoid(gate) * up).astype(BF16)
    acc_scr[...] += jnp.dot(act, wo_ref[...], preferred_element_type=F32)

    @pl.when(f == pl.num_programs(1) - 1)
    def _():
        o_ref[...] = x_ref[...] + acc_scr[...]


def _ffn(x2d, gain, w_in_bf16, w_out_bf16, tm, tf):
    n, d = x2d.shape
    dff = w_out_bf16.shape[0]
    nf = dff // tf
    return pl.pallas_call(
        _ffn_kernel,
        grid=(n // tm, nf),
        in_specs=[
            pl.BlockSpec((tm, d), lambda i, f: (i, 0)),
            pl.BlockSpec((1, d), lambda i, f: (0, 0)),
            pl.BlockSpec((d, tf), lambda i, f: (0, f)),
            pl.BlockSpec((d, tf), lambda i, f: (0, nf + f)),
            pl.BlockSpec((tf, d), lambda i, f: (f, 0)),
        ],
        out_specs=pl.BlockSpec((tm, d), lambda i, f: (i, 0)),
        out_shape=jax.ShapeDtypeStruct((n, d), F32),
        scratch_shapes=[pltpu.VMEM((tm, d), BF16), pltpu.VMEM((tm, d), F32)],
        compiler_params=_cparams(("parallel", "arbitrary")),
    )(x2d, gain.reshape(1, d), w_in_bf16, w_in_bf16, w_out_bf16)


def _out_proj_kernel(a_ref, b_ref, r_ref, w_ref, o_ref):
    half = a_ref.shape[1]
    y = jnp.dot(a_ref[...].astype(BF16), w_ref[0:half, :], preferred_element_type=F32)
    y += jnp.dot(b_ref[...].astype(BF16), w_ref[half:2 * half, :], preferred_element_type=F32)
    o_ref[...] = r_ref[...] + y


def _out_proj(a, a_blk, b, b_blk, resid, w_bf16, tm):
    n, d = resid.shape
    half = w_bf16.shape[0] // 2
    return pl.pallas_call(
        _out_proj_kernel,
        grid=(n // tm,),
        in_specs=[
            pl.BlockSpec((tm, half), lambda i: (i, a_blk)),
            pl.BlockSpec((tm, half), lambda i: (i, b_blk)),
            pl.BlockSpec((tm, d), lambda i: (i, 0)),
            pl.BlockSpec((2 * half, d), lambda i: (0, 0)),
        ],
        out_specs=pl.BlockSpec((tm, d), lambda i: (i, 0)),
        out_shape=jax.ShapeDtypeStruct((n, d), F32),
        compiler_params=_cparams(("parallel",)),
    )(a, b, resid, w_bf16)


def _flash_init(m_scr, l_scr, acc_scr):
    m_scr[...] = jnp.full(m_scr.shape, NEG, F32)
    l_scr[...] = jnp.zeros(l_scr.shape, F32)
    acc_scr[...] = jnp.zeros(acc_scr.shape, F32)


def _flash_update(s, mask, v_tile, m_scr, l_scr, acc_scr):
    s = jnp.where(mask, s, NEG)
    m_old = m_scr[...]
    m_new = jnp.maximum(m_old, jnp.max(s, axis=-1, keepdims=True))
    alpha = jnp.exp(m_old - m_new)
    p = jnp.where(mask, jnp.exp(s - m_new), 0.0)
    l_scr[...] = alpha * l_scr[...] + jnp.sum(p, axis=-1, keepdims=True)
    acc_scr[...] = alpha * acc_scr[...] + jnp.dot(p.astype(BF16), v_tile, preferred_element_type=F32)
    m_scr[...] = m_new


def _diff_lambda(lp_ref, lam_init):
    lp = lp_ref[...].astype(F32)
    a = jnp.sum(lp[0:1, :] * lp[1:2, :], axis=-1, keepdims=True)
    b = jnp.sum(lp[2:3, :] * lp[3:4, :], axis=-1, keepdims=True)
    return jnp.exp(a) - jnp.exp(b) + lam_init


def _diff_qstack(q):
    lo = lax.broadcasted_iota(jnp.int32, q.shape, 1) < HEAD_DIM
    qs = q * jnp.asarray(HEAD_DIM ** -0.5, q.dtype)
    zero = jnp.zeros_like(qs)
    return jnp.concatenate([jnp.where(lo, qs, zero), jnp.where(lo, zero, qs)], axis=0)


def _diff_finish(m_scr, l_scr, acc_scr, lam, gain, lam_init, tq):
    o = acc_scr[...] / jnp.maximum(l_scr[...], 1e-30)
    d = o[0:tq] - lam * o[tq:2 * tq]
    return _rms(d, gain) * (1.0 - lam_init)


def _diff_prompt_kernel(q_ref, k_ref, v_ref, lp_ref, sg_ref, o_ref, m_scr, l_scr, acc_scr,
                        *, tq, tk, lam_init):
    qi = pl.program_id(2)
    q0 = qi * tq
    qs = _diff_qstack(q_ref[...])
    row = lax.broadcasted_iota(jnp.int32, (2 * tq, 1), 0)
    qpos = q0 + jnp.where(row >= tq, row - tq, row)
    _flash_init(m_scr, l_scr, acc_scr)

    def body(ki, carry):
        k0 = pl.multiple_of(ki * tk, tk)
        kt = k_ref[pl.ds(k0, tk), :]
        vt = v_ref[pl.ds(k0, tk), :]
        s = _nt_dot(qs, kt)
        kpos = k0 + lax.broadcasted_iota(jnp.int32, (1, tk), 1)
        _flash_update(s, kpos <= qpos, vt, m_scr, l_scr, acc_scr)
        return carry

    lax.fori_loop(0, (q0 + tq - 1) // tk + 1, body, 0)
    lam = _diff_lambda(lp_ref, lam_init)
    o_ref[...] = _diff_finish(m_scr, l_scr, acc_scr, lam, sg_ref[...], lam_init, tq)


def _diff_prompt(projb, lam_par, subln, lam_init, tq, tk):
    b, t, _ = projb.shape
    kern = functools.partial(_diff_prompt_kernel, tq=tq, tk=tk, lam_init=lam_init)
    kblk = C_DK // LANES
    vblk = C_DV // LANES
    return pl.pallas_call(
        kern,
        grid=(b, DA_HEADS, t // tq),
        in_specs=[
            pl.BlockSpec((None, tq, LANES), lambda bi, h, qi: (bi, qi, h)),
            pl.BlockSpec((None, t, LANES), lambda bi, h, qi: (bi, 0, kblk + h)),
            pl.BlockSpec((None, t, LANES), lambda bi, h, qi: (bi, 0, vblk + h)),
            pl.BlockSpec((4, HEAD_DIM), lambda bi, h, qi: (0, 0)),
            pl.BlockSpec((1, DA_VDIM), lambda bi, h, qi: (0, 0)),
        ],
        out_specs=pl.BlockSpec((None, tq, LANES), lambda bi, h, qi: (bi, qi, h)),
        out_shape=jax.ShapeDtypeStruct((b, t, DA_HEADS * DA_VDIM), F32),
        scratch_shapes=[pltpu.VMEM((2 * tq, 1), F32), pltpu.VMEM((2 * tq, 1), F32),
                        pltpu.VMEM((2 * tq, LANES), F32)],
        compiler_params=_cparams(("parallel", "parallel", "arbitrary")),
    )(projb, projb, projb, lam_par, subln.reshape(1, DA_VDIM))


def _page_copies(cache_ref, e, pt_ref, seq, page0, n_pages, lane0, n_lanes, buf, sem, page_size):
    copies = []
    for j in range(n_pages):
        page = pt_ref[seq, page0 + j]
        copies.append(pltpu.make_async_copy(
            cache_ref.at[e, page, :, pl.ds(lane0, n_lanes)],
            buf.at[pl.ds(j * page_size, page_size), :],
            sem.at[j]))
    return copies


def _diff_sample_kernel(pt_ref, q_ref, tail_ref, lp_ref, sg_ref, cache_ref, o_ref,
                        buf, sem, m_scr, l_scr, acc_scr,
                        *, e, n_chunks, pages_per_chunk, page_size, past_len, tk, lam_init):
    s_idx = pl.program_id(0)
    c_idx = pl.program_id(1)
    tq = SAMPLE_ROWS
    width = DA_HEADS * DA_VDIM
    row = lax.broadcasted_iota(jnp.int32, (2 * tq, 1), 0)
    qpos = past_len + jnp.where(row >= tq, row - tq, row)

    @pl.when(c_idx == 0)
    def _():
        _flash_init(m_scr, l_scr, acc_scr)

    def update(h, kt, vt, kpos):
        qs = _diff_qstack(q_ref[:, h * LANES:(h + 1) * LANES])
        hs = slice(h * 2 * tq, (h + 1) * 2 * tq)
        _flash_update(_nt_dot(qs, kt.astype(BF16)), kpos <= qpos, vt.astype(BF16),
                      m_scr.at[hs], l_scr.at[hs], acc_scr.at[hs])

    @pl.when(c_idx < n_chunks)
    def _():
        copies = _page_copies(cache_ref, e, pt_ref, s_idx, c_idx * pages_per_chunk, pages_per_chunk,
                              0, 2 * width, buf, sem, page_size)
        for cp in copies:
            cp.start()
        for cp in copies:
            cp.wait()
        chunk_rows = pages_per_chunk * page_size
        for h in range(DA_HEADS):
            for r0 in range(0, chunk_rows, tk):
                kpos = (c_idx * chunk_rows + r0) + lax.broadcasted_iota(jnp.int32, (1, tk), 1)
                update(h, buf[r0:r0 + tk, h * LANES:(h + 1) * LANES],
                       buf[r0:r0 + tk, width + h * LANES:width + (h + 1) * LANES], kpos)

    @pl.when(c_idx == n_chunks)
    def _():
        n_tail = tail_ref.shape[0]
        kpos = past_len + lax.broadcasted_iota(jnp.int32, (1, n_tail), 1)
        lam = _diff_lambda(lp_ref, lam_init)
        for h in range(DA_HEADS):
            update(h, tail_ref[:, h * LANES:(h + 1) * LANES],
                   tail_ref[:, width + h * LANES:width + (h + 1) * LANES], kpos)
            hs = slice(h * 2 * tq, (h + 1) * 2 * tq)
            o_ref[:, h * LANES:(h + 1) * LANES] = _diff_finish(
                m_scr.at[hs], l_scr.at[hs], acc_scr.at[hs], lam, sg_ref[...], lam_init, tq)


def _diff_sample(projb_s, tail, cache4, e, page_table, lam_par, subln, lam_init, past_len):
    s, tq, _ = projb_s.shape
    n_pages = page_table.shape[1]
    page_size = cache4.shape[2]
    pages_per_chunk = math.gcd(n_pages, 16)
    n_chunks = n_pages // pages_per_chunk
    width = DA_HEADS * DA_VDIM
    tk = math.gcd(pages_per_chunk * page_size, 512)
    kern = functools.partial(_diff_sample_kernel, e=e, n_chunks=n_chunks, pages_per_chunk=pages_per_chunk,
                             page_size=page_size, past_len=past_len, tk=tk, lam_init=lam_init)
    grid_spec = pltpu.PrefetchScalarGridSpec(
        num_scalar_prefetch=1,
        grid=(s, n_chunks + 1),
        in_specs=[
            pl.BlockSpec((None, tq, width), lambda si, ci, pt: (si, 0, 0)),
            pl.BlockSpec((None, tail.shape[1], 2 * width), lambda si, ci, pt: (si, 0, 0)),
            pl.BlockSpec((4, HEAD_DIM), lambda si, ci, pt: (0, 0)),
            pl.BlockSpec((1, DA_VDIM), lambda si, ci, pt: (0, 0)),
            pl.BlockSpec(memory_space=pl.ANY),
        ],
        out_specs=pl.BlockSpec((None, tq, width), lambda si, ci, pt: (si, 0, 0)),
        scratch_shapes=[
            pltpu.VMEM((pages_per_chunk * page_size, 2 * width), F32),
            pltpu.SemaphoreType.DMA((pages_per_chunk,)),
            pltpu.VMEM((DA_HEADS * 2 * tq, 1), F32),
            pltpu.VMEM((DA_HEADS * 2 * tq, 1), F32),
            pltpu.VMEM((DA_HEADS * 2 * tq, LANES), F32),
        ],
    )
    return pl.pallas_call(
        kern,
        grid_spec=grid_spec,
        out_shape=jax.ShapeDtypeStruct((s, tq, width), F32),
        compiler_params=_cparams(("arbitrary", "arbitrary")),
    )(page_table, projb_s, tail, lam_par, subln.reshape(1, DA_VDIM), cache4)


def _compress_compute(xk_ref, xv_ref, pe_ref, w1_ref, w2_ref, kg_ref, kcb_ref, vcb_ref, nb):
    def body(l, accs):
        pe = pe_ref[pl.ds(l, 1), :]
        ak = (xk_ref[pl.ds(l, nb, stride=NSA_BLOCK), :] + pe[:, 0:LANES]).astype(BF16)
        av = (xv_ref[pl.ds(l, nb, stride=NSA_BLOCK), :] + pe[:, LANES:2 * LANES]).astype(BF16)
        hk = jnp.dot(ak, w1_ref[0, l], preferred_element_type=F32)
        hv = jnp.dot(av, w1_ref[1, l], preferred_element_type=F32)
        return accs[0] + hk, accs[1] + hv

    zero = jnp.zeros((nb, 2 * NSA_CMP_HID), F32)
    hk, hv = lax.fori_loop(0, NSA_BLOCK, body, (zero, zero))
    ck = jnp.dot(jax.nn.gelu(hk).astype(BF16), w2_ref[0], preferred_element_type=F32)
    cv = jnp.dot(jax.nn.gelu(hv).astype(BF16), w2_ref[1], preferred_element_type=F32)
    ck = _seg_rms(ck, kg_ref[...])
    for ref in (kcb_ref, vcb_ref):
        ref[...] = jnp.zeros(ref.shape, ref.dtype)
    kcb_ref[0:nb, :] = ck.astype(BF16)
    vcb_ref[0:nb, :] = cv.astype(BF16)


def _compress_prompt_kernel(xk_ref, xv_ref, pe_ref, w1_ref, w2_ref, kg_ref, kcb_ref, vcb_ref, *, nb):
    _compress_compute(xk_ref, xv_ref, pe_ref, w1_ref, w2_ref, kg_ref, kcb_ref, vcb_ref, nb)


def _compress_specs(nbp):
    hid2 = 2 * NSA_CMP_HID
    w_specs = [
        pl.BlockSpec((NSA_BLOCK, 2 * LANES), lambda *a: (0, 0)),
        pl.BlockSpec((2, NSA_BLOCK, LANES, hid2), lambda *a: (0, 0, 0, 0)),
        pl.BlockSpec((2, hid2, LANES), lambda *a: (0, 0, 0)),
        pl.BlockSpec((1, LANES), lambda *a: (0, 0)),
    ]
    out_specs = [pl.BlockSpec((None, nbp, LANES), lambda bi, *a: (bi, 0, 0))] * 2
    return w_specs, out_specs


def _compress_out_shape(nbat, nbp):
    return [jax.ShapeDtypeStruct((nbat, nbp, LANES), BF16)] * 2


def _compress_prompt(projf, cw):
    b, t, _ = projf.shape
    nb = t // NSA_BLOCK
    nbp = -(-nb // LANES) * LANES
    w_specs, out_specs = _compress_specs(nbp)
    return pl.pallas_call(
        functools.partial(_compress_prompt_kernel, nb=nb),
        grid=(b,),
        in_specs=[pl.BlockSpec((None, t, LANES), lambda bi: (bi, 0, C_NKV // LANES)),
                  pl.BlockSpec((None, t, LANES), lambda bi: (bi, 0, C_NKV // LANES + 1))] + w_specs,
        out_specs=out_specs,
        out_shape=_compress_out_shape(b, nbp),
        compiler_params=_cparams(("parallel",)),
    )(projf, projf, *cw)


def _compress_sample_kernel(pt_ref, tail_ref, pe_ref, w1_ref, w2_ref, kg_ref, cache_ref,
                            kcb_ref, vcb_ref, kbuf, vbuf, ksem, vsem, *, e, n_pages, page_size, nb):
    s_idx = pl.program_id(0)
    copies = (_page_copies(cache_ref, e, pt_ref, s_idx, 0, n_pages, 0, LANES, kbuf, ksem, page_size)
              + _page_copies(cache_ref, e, pt_ref, s_idx, 0, n_pages, LANES, LANES, vbuf, vsem, page_size))
    for cp in copies:
        cp.start()
    kbuf[n_pages * page_size:, :] = tail_ref[:, 0:LANES]
    vbuf[n_pages * page_size:, :] = tail_ref[:, LANES:2 * LANES]
    for cp in copies:
        cp.wait()
    _compress_compute(kbuf, vbuf, pe_ref, w1_ref, w2_ref, kg_ref, kcb_ref, vcb_ref, nb)


def _compress_sample(tail, cache3, e, page_table, cw):
    s = tail.shape[0]
    n_pages = page_table.shape[1]
    page_size = cache3.shape[2]
    rows = n_pages * page_size + tail.shape[1]
    nb = rows // NSA_BLOCK
    nbp = -(-nb // LANES) * LANES
    w_specs, out_specs = _compress_specs(nbp)
    grid_spec = pltpu.PrefetchScalarGridSpec(
        num_scalar_prefetch=1,
        grid=(s,),
        in_specs=[pl.BlockSpec((None, tail.shape[1], 2 * LANES), lambda si, pt: (si, 0, 0))] + w_specs
        + [pl.BlockSpec(memory_space=pl.ANY)],
        out_specs=out_specs,
        scratch_shapes=[pltpu.VMEM((rows, LANES), F32), pltpu.VMEM((rows, LANES), F32),
                        pltpu.SemaphoreType.DMA((n_pages,)), pltpu.SemaphoreType.DMA((n_pages,))],
    )
    return pl.pallas_call(
        functools.partial(_compress_sample_kernel, e=e, n_pages=n_pages, page_size=page_size, nb=nb),
        grid_spec=grid_spec,
        out_shape=_compress_out_shape(s, nbp),
        compiler_params=_cparams(("arbitrary",)),
    )(page_table, tail, *cw, cache3)


def _nsa_compute(q, gate_logits, kc_ref, vc_ref, ksv_ref, kwv_ref, m_s, l_s, a_s, m_w, l_w, a_w,
                 *, tq, q0, n_blk, n_sel_rows, tk, wbase, n_win_rows, tkw):
    nh = NSA_HEADS
    rows = nh * tq
    blk_shift = NSA_BLOCK.bit_length() - 1
    assert tq & (tq - 1) == 0 and 1 << blk_shift == NSA_BLOCK
    lane = lax.broadcasted_iota(jnp.int32, (tq, LANES), 1)
    qf = q.astype(F32) * (HEAD_DIM ** -0.5)
    parts = []
    for hh in range(nh):
        g = hh // NSA_REP
        blk = qf[:, (hh // 2) * LANES:(hh // 2 + 1) * LANES]
        if hh % 2 != g:
            blk = pltpu.roll(blk, HEAD_DIM, 1)
        in_seg = (lane >= g * HEAD_DIM) & (lane < (g + 1) * HEAD_DIM)
        parts.append(jnp.where(in_seg, blk, 0.0).astype(BF16))
    qs = jnp.concatenate(parts, axis=0)

    rowi = lax.broadcasted_iota(jnp.int32, (rows, 1), 0)
    qpos = q0 + (rowi & (tq - 1))
    qpos_t = q0 + lax.broadcasted_iota(jnp.int32, (tq, 1), 0)

    nbp = kc_ref.shape[0]
    sc = _nt_dot(qs, kc_ref[...])
    blk_i = lax.broadcasted_iota(jnp.int32, (1, nbp), 1)
    blk_f = blk_i.astype(F32)
    cmp_ok = blk_i < ((qpos + 1) >> blk_shift)
    sc = jnp.where(cmp_ok, sc, NEG)
    mc = jnp.max(sc, axis=-1, keepdims=True)
    ec = jnp.where(cmp_ok, jnp.exp(sc - mc), 0.0)
    pc = ec / jnp.maximum(jnp.sum(ec, axis=-1, keepdims=True), 1e-30)
    o_cmp = jnp.dot(pc.astype(BF16), vc_ref[...], preferred_element_type=F32)

    cur = qpos_t >> blk_shift
    cand = blk_i < cur
    n_pick = min(NSA_TOPK - 1, n_blk)
    sel = []
    for g in range(NSA_GROUPS):
        imp = pc[(g * NSA_REP) * tq:(g * NSA_REP + 1) * tq]
        for r in range(1, NSA_REP):
            imp = imp + pc[(g * NSA_REP + r) * tq:(g * NSA_REP + r + 1) * tq]
        x = jnp.where(cand, imp, -1.0)
        x = jnp.where(blk_i < n_blk, x, -2.0)
        picked = jnp.zeros((tq, nbp), F32)
        for _ in range(n_pick):
            mx = jnp.max(x, axis=-1, keepdims=True)
            first = jnp.min(jnp.where(x == mx, blk_f, float(nbp)), axis=-1, keepdims=True)
            hit = blk_f == first
            picked = jnp.where(hit, 1.0, picked)
            x = jnp.where(hit, -3.0, x)
        chosen = jnp.where(cand, picked, 0.0)
        sel.append(jnp.where(blk_i == cur, 1.0, chosen).astype(BF16))

    _flash_init(m_s, l_s, a_s)
    blk_row = lax.broadcasted_iota(jnp.int32, (nbp, tk), 0)

    def sel_body(ki, carry):
        k0 = pl.multiple_of(ki * tk, tk)
        kt = ksv_ref[pl.ds(k0, tk), 0:LANES].astype(BF16)
        vt = ksv_ref[pl.ds(k0, tk), LANES:2 * LANES].astype(BF16)
        kpos = k0 + lax.broadcasted_iota(jnp.int32, (1, tk), 1)
        expand = (blk_row == (kpos >> blk_shift)).astype(BF16)
        hits = []
        for g in range(NSA_GROUPS):
            hits += [jnp.dot(sel[g], expand, preferred_element_type=F32)] * NSA_REP
        mask = (jnp.concatenate(hits, axis=0) > 0.5) & (kpos <= qpos)
        _flash_update(_nt_dot(qs, kt), mask, vt, m_s, l_s, a_s)
        return carry

    last_sel = jnp.minimum((q0 + tq - 1) // tk, n_sel_rows // tk - 1)
    lax.fori_loop(0, last_sel + 1, sel_body, 0)

    _flash_init(m_w, l_w, a_w)

    def win_body(ki, carry):
        k0 = pl.multiple_of(ki * tkw, tkw)
        kt = kwv_ref[pl.ds(k0, tkw), 0:LANES].astype(BF16)
        vt = kwv_ref[pl.ds(k0, tkw), LANES:2 * LANES].astype(BF16)
        kpos = wbase + k0 + lax.broadcasted_iota(jnp.int32, (1, tkw), 1)
        mask = (kpos <= qpos) & (kpos >= qpos - NSA_WINDOW) & (kpos >= 0)
        _flash_update(_nt_dot(qs, kt), mask, vt, m_w, l_w, a_w)
        return carry

    lo_tile = jnp.maximum(q0 - NSA_WINDOW - wbase, 0) // tkw
    hi_tile = jnp.minimum((q0 + tq - 1 - wbase) // tkw, n_win_rows // tkw - 1)
    lax.fori_loop(lo_tile, hi_tile + 1, win_body, 0)

    o_sel = a_s[...] / jnp.maximum(l_s[...], 1e-30)
    o_win = a_w[...] / jnp.maximum(l_w[...], 1e-30)
    gates = jax.nn.sigmoid(gate_logits)
    lo_half = lane < HEAD_DIM
    heads = []
    for hh in range(nh):
        g = hh // NSA_REP
        rs = slice(hh * tq, (hh + 1) * tq)
        o = (gates[:, 3 * hh:3 * hh + 1] * o_cmp[rs] + gates[:, 3 * hh + 1:3 * hh + 2] * o_sel[rs]
             + gates[:, 3 * hh + 2:3 * hh + 3] * o_win[rs])
        if hh % 2 != g:
            o = pltpu.roll(o, HEAD_DIM, 1)
        heads.append(o)
    return jnp.concatenate([jnp.where(lo_half, heads[2 * j], heads[2 * j + 1]) for j in range(nh // 2)], axis=1)


def _nsa_prompt_kernel(q_ref, g_ref, kc_ref, vc_ref, ksv_ref, kwv_ref, o_ref,
                       m_s, l_s, a_s, m_w, l_w, a_w, *, tq, tk, tkw, n_blk, t):
    q0 = pl.program_id(1) * tq
    o_ref[...] = _nsa_compute(q_ref[...], g_ref[...], kc_ref, vc_ref, ksv_ref, kwv_ref,
                              m_s, l_s, a_s, m_w, l_w, a_w, tq=tq, q0=q0, n_blk=n_blk,
                              n_sel_rows=t, tk=tk, wbase=0, n_win_rows=t, tkw=tkw)


def _nsa_scratch(tq):
    rows = NSA_HEADS * tq
    one = [pltpu.VMEM((rows, 1), F32), pltpu.VMEM((rows, 1), F32), pltpu.VMEM((rows, LANES), F32)]
    return one + one


def _nsa_prompt(projf, projb, kcb, vcb, tq, tk):
    b, t, _ = projb.shape
    nbp = kcb.shape[1]
    tkw = min(tq, LANES)
    kern = functools.partial(_nsa_prompt_kernel, tq=tq, tk=tk, tkw=tkw, n_blk=t // NSA_BLOCK, t=t)
    q_blk = C_NQ // 512
    sv_blk = (C_NKV + 2 * LANES) // (2 * LANES)
    return pl.pallas_call(
        kern,
        grid=(b, t // tq),
        in_specs=[
            pl.BlockSpec((None, tq, 512), lambda bi, qi: (bi, qi, q_blk)),
            pl.BlockSpec((None, tq, LANES), lambda bi, qi: (bi, qi, C_NG // LANES)),
            pl.BlockSpec((None, nbp, LANES), lambda bi, qi: (bi, 0, 0)),
            pl.BlockSpec((None, nbp, LANES), lambda bi, qi: (bi, 0, 0)),
            pl.BlockSpec((None, t, 2 * LANES), lambda bi, qi: (bi, 0, sv_blk)),
            pl.BlockSpec((None, t, 2 * LANES), lambda bi, qi: (bi, 0, sv_blk + 1)),
        ],
        out_specs=pl.BlockSpec((None, tq, 512), lambda bi, qi: (bi, qi, 0)),
        out_shape=jax.ShapeDtypeStruct((b, t, NSA_HEADS * HEAD_DIM), F32),
        scratch_shapes=_nsa_scratch(tq),
        compiler_params=_cparams(("parallel", "arbitrary")),
    )(projb, projf, kcb, vcb, projb, projb)


def _nsa_sample_kernel(pt_ref, q_ref, g_ref, kc_ref, vc_ref, tail_ref, kwv_ref, cache_ref, o_ref,
                       buf, sem, m_s, l_s, a_s, m_w, l_w, a_w,
                       *, e, n_pages, page_size, past_len, n_blk, tk, tkw, wbase):
    s_idx = pl.program_id(0)
    copies = _page_copies(cache_ref, e, pt_ref, s_idx, 0, n_pages, 2 * LANES, 2 * LANES, buf, sem, page_size)
    for cp in copies:
        cp.start()
    buf[n_pages * page_size:, :] = tail_ref[...]
    for cp in copies:
        cp.wait()
    o_ref[...] = _nsa_compute(q_ref[...], g_ref[...], kc_ref, vc_ref, buf, kwv_ref,
                              m_s, l_s, a_s, m_w, l_w, a_w, tq=SAMPLE_ROWS, q0=past_len, n_blk=n_blk,
                              n_sel_rows=buf.shape[0], tk=tk, wbase=wbase, n_win_rows=kwv_ref.shape[0], tkw=tkw)


def _nsa_sample(projf_s, projb_s, kcb, vcb, sel_tail, kwv, cache3, e, page_table, past_len, n_valid):
    s, tq, _ = projb_s.shape
    n_pages = page_table.shape[1]
    page_size = cache3.shape[2]
    rows = n_pages * page_size + sel_tail.shape[1]
    nbp = kcb.shape[1]
    n_blk = -(-(past_len + n_valid) // NSA_BLOCK)
    tk = math.gcd(rows, 512)
    tkw = math.gcd(kwv.shape[1], LANES)
    wbase = past_len - (kwv.shape[1] - sel_tail.shape[1])
    kern = functools.partial(_nsa_sample_kernel, e=e, n_pages=n_pages, page_size=page_size, past_len=past_len,
                             n_blk=n_blk, tk=tk, tkw=tkw, wbase=wbase)
    grid_spec = pltpu.PrefetchScalarGridSpec(
        num_scalar_prefetch=1,
        grid=(s,),
        in_specs=[
            pl.BlockSpec((None, tq, 512), lambda si, pt: (si, 0, C_NQ // 512)),
            pl.BlockSpec((None, tq, LANES), lambda si, pt: (si, 0, C_NG // LANES)),
            pl.BlockSpec((None, nbp, LANES), lambda si, pt: (si, 0, 0)),
            pl.BlockSpec((None, nbp, LANES), lambda si, pt: (si, 0, 0)),
            pl.BlockSpec((None, sel_tail.shape[1], 2 * LANES), lambda si, pt: (si, 0, 0)),
            pl.BlockSpec((None, kwv.shape[1], 2 * LANES), lambda si, pt: (si, 0, 0)),
            pl.BlockSpec(memory_space=pl.ANY),
        ],
        out_specs=pl.BlockSpec((None, tq, 512), lambda si, pt: (si, 0, 0)),
        scratch_shapes=[pltpu.VMEM((rows, 2 * LANES), F32), pltpu.SemaphoreType.DMA((n_pages,))]
        + _nsa_scratch(tq),
    )
    return pl.pallas_call(
        kern,
        grid_spec=grid_spec,
        out_shape=jax.ShapeDtypeStruct((s, tq, NSA_HEADS * HEAD_DIM), F32),
        compiler_params=_cparams(("arbitrary",)),
    )(page_table, projb_s, projf_s, kcb, vcb, sel_tail, kwv, cache3)


def _split3(x):
    hi = x.astype(BF16)
    r1 = x - hi.astype(F32)
    mid = r1.astype(BF16)
    lo = (r1 - mid.astype(F32)).astype(BF16)
    return hi, mid, lo


def _hgrn_kernel(q_ref, f_ref, i_ref, g_ref, lb_ref, ng_ref, s0_ref, o_ref, sfin_ref,
                 st_scr, cum_scr, k_scr, v_scr, q_scr, *, chunk, n_chunks, n_valid):
    ti = pl.program_id(2)

    @pl.when(ti == 0)
    def _():
        st_scr[...] = s0_ref[...]

    lb = lb_ref[...]
    tri = (lax.broadcasted_iota(jnp.int32, (chunk, chunk), 0)
           >= lax.broadcasted_iota(jnp.int32, (chunk, chunk), 1)).astype(BF16)
    row = lax.broadcasted_iota(jnp.int32, (chunk, 1), 0)

    def chunk_body(c, carry):
        r0 = pl.multiple_of(c * chunk, chunk)
        qr = q_ref[pl.ds(r0, chunk), :]
        q = qr * jax.nn.sigmoid(qr) * (HG_DK ** -0.5)
        fg = lb + (1.0 - lb) * jax.nn.sigmoid(f_ref[pl.ds(r0, chunk), :])
        logf = jnp.log(fg)
        k = 1.0 - fg
        v = i_ref[pl.ds(r0, chunk), :]
        if n_valid is not None:
            live = (ti * (chunk * n_chunks) + r0 + row) < n_valid
            logf = jnp.where(live, logf, 0.0)
            k = jnp.where(live, k, 0.0)
        cum = sum(jnp.dot(tri, part, preferred_element_type=F32) for part in _split3(logf))
        cum_scr[...] = cum
        k_scr[...] = k
        v_scr[...] = v
        q_scr[...] = q
        st = st_scr[...]
        inter = _nt_dot((q * jnp.exp(cum)).astype(BF16), st.astype(BF16))

        def s_body(s, o):
            cs = cum_scr[pl.ds(s, 1), :]
            dec = jnp.where(row >= s, jnp.exp(jnp.minimum(cum_scr[...] - cs, 0.0)), 0.0)
            a = jnp.sum(q_scr[...] * dec * k_scr[pl.ds(s, 1), :], axis=-1, keepdims=True)
            return o + a * v_scr[pl.ds(s, 1), :]

        o = lax.fori_loop(0, chunk, s_body, inter)
        last = cum[chunk - 1:chunk, :]
        kd = k * jnp.exp(last - cum)
        st_scr[...] = st * jnp.exp(last) + jnp.dot(v.T.astype(BF16), kd.astype(BF16),
                                                    preferred_element_type=F32)
        gr = g_ref[pl.ds(r0, chunk), :]
        o_ref[pl.ds(r0, chunk), :] = _rms(o, ng_ref[...]) * (gr * jax.nn.sigmoid(gr))
        return carry

    lax.fori_loop(0, n_chunks, chunk_body, 0)

    @pl.when(ti == pl.num_programs(2) - 1)
    def _():
        sfin_ref[...] = st_scr[...]


def _hgrn(projf, lb, norm_gain, s0t, chunk, tt, n_valid):
    b, t, _ = projf.shape
    dk = HG_DK
    kern = functools.partial(_hgrn_kernel, chunk=chunk, n_chunks=tt // chunk, n_valid=n_valid)
    col = lambda j: pl.BlockSpec((None, tt, dk), lambda bi, h, ti: (bi, ti, j * HG_HEADS + h))
    vec = pl.BlockSpec((1, dk), lambda bi, h, ti: (0, h))
    st_spec = pl.BlockSpec((None, None, dk, dk), lambda bi, h, ti: (bi, h, 0, 0))
    return pl.pallas_call(
        kern,
        grid=(b, HG_HEADS, t // tt),
        in_specs=[col(0), col(1), col(2), col(3), vec, vec, st_spec],
        out_specs=[pl.BlockSpec((None, tt, dk), lambda bi, h, ti: (bi, ti, h)), st_spec],
        out_shape=[jax.ShapeDtypeStruct((b, t, HG_HEADS * dk), F32),
                   jax.ShapeDtypeStruct((b, HG_HEADS, dk, dk), F32)],
        scratch_shapes=[pltpu.VMEM((dk, dk), F32)] + [pltpu.VMEM((chunk, dk), F32)] * 4,
        compiler_params=_cparams(("parallel", "parallel", "arbitrary")),
    )(projf, projf, projf, projf, lb.reshape(1, -1), norm_gain.reshape(1, -1), s0t)


def _row_tile(n, pref):
    return math.gcd(n, pref)


def _pad_rows(x, rows):
    return jnp.pad(x, ((0, 0), (0, rows - x.shape[1]), (0, 0)))


def _even_weights(w_in, qk_a, qk_b, pe, w1, w2):
    w_pad = jnp.pad(w_in, ((0, 0), (0, EVEN_IN_PAD - EVEN_IN))).astype(BF16)
    ones = jnp.ones((HEAD_DIM,), F32)
    segs = [qk_a[0]] * 8 + [qk_a[1]] * 8 + [ones] * 8 + [qk_b[0]] * 8 + [ones] * 4 + [qk_b[2]] * 2 \
        + [ones] * 2 + [qk_b[3]] * 2 + [ones] * 4
    colgain = jnp.concatenate(segs).reshape(1, EVEN_IN_PAD).astype(F32)
    pe_cat = jnp.concatenate([pe[0], pe[0], pe[1], pe[1]], axis=-1).astype(F32)
    z1 = jnp.zeros_like(w1)
    w1bd = jnp.concatenate([jnp.concatenate([w1, z1], axis=-1), jnp.concatenate([z1, w1], axis=-1)],
                           axis=-2).astype(BF16)
    z2 = jnp.zeros_like(w2)
    w2bd = jnp.concatenate([jnp.concatenate([w2, z2], axis=-1), jnp.concatenate([z2, w2], axis=-1)],
                           axis=-2).astype(BF16)
    kgain = jnp.concatenate([qk_b[1], qk_b[1]]).reshape(1, LANES).astype(F32)
    return w_pad, colgain, (pe_cat, w1bd, w2bd, kgain)


def kernel(x_prompt, x_sample, cache_diff_kv, cache_nsa_kv, cache_nsa_win, state_hgrn, page_table,
           norm_mix, norm_ffn, w_ffn_in, w_ffn_out, w_in_even, w_out_even, diff_qk_gain, diff_lambda,
           diff_subln_gain, nsa_qk_gain, nsa_cmp_pe, nsa_cmp_w1, nsa_cmp_w2, w_in_odd, w_out_odd,
           hgrn_norm_gain, hgrn_lb_logits):
    b, t, d = x_prompt.shape
    s, ts, _ = x_sample.shape
    depth = norm_mix.shape[0]
    n_even = cache_diff_kv.shape[0]
    n_pool, page_size = cache_diff_kv.shape[1], cache_diff_kv.shape[2]
    past_len = page_table.shape[1] * page_size
    n_buf = cache_nsa_win.shape[2]
    tsp = SAMPLE_ROWS
    assert ts <= tsp and d == D_MODEL

    lbw = jax.nn.softmax(hgrn_lb_logits.astype(F32), axis=0)
    lower_bounds = jnp.cumsum(lbw, axis=0) - lbw[0]

    hp = x_prompt.reshape(b * t, d)
    hs = _pad_rows(x_sample, tsp).reshape(s * tsp, d)
    tm_p = _row_tile(b * t, 512)
    tm_s = _row_tile(s * tsp, 256)
    cache_diff4 = cache_diff_kv.reshape(n_even, n_pool, page_size, 2 * DA_HEADS * DA_VDIM)
    cache_nsa3 = cache_nsa_kv.reshape(n_even, n_pool, page_size, 4 * NSA_GROUPS * HEAD_DIM)
    cache_win = cache_nsa_win.reshape(n_even, s, n_buf, 2 * NSA_GROUPS * HEAD_DIM)
    live = (jnp.arange(tsp) < ts)[None, :, None]

    dkv_p, dkv_s, nkv_p, nkv_s, win_p, win_s, hg_p, hg_s = [], [], [], [], [], [], [], []
    for l in range(depth):
        if l % 2 == 0:
            e = l // 2
            lam_init = 0.8 - 0.6 * math.exp(-0.3 * l)
            w_pad, colgain, cw = _even_weights(w_in_even[e], diff_qk_gain[e], nsa_qk_gain[e],
                                               nsa_cmp_pe[e], nsa_cmp_w1[e], nsa_cmp_w2[e])
            w_out = w_out_even[e].astype(BF16)
            pf, pb = _norm_proj(hp, norm_mix[l], w_pad, colgain, EVEN_NORM_BLOCKS, tm_p)
            pf3, pb3 = pf.reshape(b, t, -1), pb.reshape(b, t, -1)
            da = _diff_prompt(pb3, diff_lambda[e], diff_subln_gain[e], lam_init,
                              tq=_row_tile(t, 256), tk=_row_tile(t, 512))
            kcb, vcb = _compress_prompt(pf3, cw)
            onsa = _nsa_prompt(pf3, pb3, kcb, vcb, tq=_row_tile(t, 128), tk=_row_tile(t, 256))
            hp = _out_proj(da.reshape(b * t, -1), 0, onsa.reshape(b * t, -1), 0, hp, w_out, tm_p)
            dkv_p.append(pf3[:, :, C_DK:C_NQ].reshape(b, t, 2, DA_HEADS, DA_VDIM))
            nkv_p.append(pf3[:, :, C_NKV:C_NKV + 512].reshape(b, t, 4, NSA_GROUPS, HEAD_DIM))
            nw = min(NSA_WINDOW, t)
            win_p.append(pf3[:, t - nw:, C_NKV + 512:C_NG].reshape(b, nw, 2, NSA_GROUPS, HEAD_DIM))
            sf, sb = _norm_proj(hs, norm_mix[l], w_pad, colgain, EVEN_NORM_BLOCKS, tm_s)
            sf3, sb3 = sf.reshape(s, tsp, -1), sb.reshape(s, tsp, -1)
            diff_tail = _pad_rows(sf3[:, :, C_DK:C_NQ], LANES)
            da_s = _diff_sample(sb3, diff_tail, cache_diff4, e, page_table, diff_lambda[e],
                                diff_subln_gain[e], lam_init, past_len)
            cmp_tail = _pad_rows(jnp.where(live, sf3[:, :, C_NKV:C_NKV + 256], 0.0), LANES)
            kcb_s, vcb_s = _compress_sample(cmp_tail, cache_nsa3, e, page_table, cw)
            sel_tail = _pad_rows(sf3[:, :, C_NKV + 256:C_NKV + 512], LANES)
            new_win = sf3[:, :, C_NKV + 512:C_NG]
            kwv = jnp.concatenate([cache_win[e], _pad_rows(new_win, LANES)], axis=1)
            onsa_s = _nsa_sample(sf3, sb3, kcb_s, vcb_s, sel_tail, kwv, cache_nsa3, e, page_table,
                                 past_len, ts)
            hs = _out_proj(da_s.reshape(s * tsp, -1), 0, onsa_s.reshape(s * tsp, -1), 0, hs, w_out, tm_s)
            dkv_s.append(sf3[:, :ts, C_DK:C_NQ].reshape(s, ts, 2, DA_HEADS, DA_VDIM))
            nkv_s.append(sf3[:, :ts, C_NKV:C_NKV + 512].reshape(s, ts, 4, NSA_GROUPS, HEAD_DIM))
            win_all = jnp.concatenate([cache_win[e], new_win[:, :ts]], axis=1)[:, -n_buf:]
            win_s.append(win_all.reshape(s, n_buf, 2, NSA_GROUPS, HEAD_DIM))
        else:
            r = l // 2
            w_in = w_in_odd[r].astype(BF16)
            w_out = w_out_odd[r].astype(BF16)
            ones = jnp.ones((1, ODD_IN), F32)
            pf, _ = _norm_proj(hp, norm_mix[l], w_in, ones, (), tm_p)
            chunk = math.gcd(t, HG_CHUNK)
            o_p, st_p = _hgrn(pf.reshape(b, t, -1), lower_bounds[l], hgrn_norm_gain[r],
                              jnp.zeros((b, HG_HEADS, HG_DK, HG_DK), F32), chunk, _row_tile(t, 512), None)
            hp = _out_proj(o_p.reshape(b * t, -1), 0, o_p.reshape(b * t, -1), 1, hp, w_out, tm_p)
            hg_p.append(jnp.swapaxes(st_p, -1, -2))
            sf, _ = _norm_proj(hs, norm_mix[l], w_in, ones, (), tm_s)
            o_s, st_s = _hgrn(sf.reshape(s, tsp, -1), lower_bounds[l], hgrn_norm_gain[r],
                              jnp.swapaxes(state_hgrn[r].astype(F32), -1, -2), tsp, tsp, ts)
            hs = _out_proj(o_s.reshape(s * tsp, -1), 0, o_s.reshape(s * tsp, -1), 1, hs, w_out, tm_s)
            hg_s.append(jnp.swapaxes(st_s, -1, -2))
        w_fi = w_ffn_in[l].astype(BF16)
        w_fo = w_ffn_out[l].astype(BF16)
        tf = w_fo.shape[0] // 2
        hp = _ffn(hp, norm_ffn[l], w_fi, w_fo, tm_p, tf)
        hs = _ffn(hs, norm_ffn[l], w_fi, w_fo, tm_s, tf)

    y_s = hs.reshape(s, tsp, d)[:, :ts]
    return (hp.reshape(b, t, d), y_s, jnp.stack(dkv_p), jnp.stack(dkv_s), jnp.stack(nkv_p), jnp.stack(nkv_s),
            jnp.stack(win_p), jnp.stack(win_s), jnp.stack(hg_p), jnp.stack(hg_s))
```

```python
import functools
import math

import jax
import jax.numpy as jnp
from jax import lax
from jax.experimental import pallas as pl
from jax.experimental.pallas import tpu as pltpu

F32 = jnp.float32
BF16 = jnp.bfloat16

D_MODEL = 1024
HEAD_DIM = 64
DA_HEADS = 4
DA_VDIM = 2 * HEAD_DIM
NSA_HEADS = 8
NSA_GROUPS = 2
NSA_REP = NSA_HEADS // NSA_GROUPS
NSA_BLOCK = 64
NSA_TOPK = 16
NSA_WINDOW = 512
NSA_CMP_HID = 2 * HEAD_DIM
HG_HEADS = 8
HG_DK = D_MODEL // HG_HEADS
HG_CHUNK = 64
EPS = 1e-6
NEG = -1e30
M_FLOOR = -1e29
LOG2E = 1.4426950408889634

LANES = 128
SUBLANES = 8
VMEM_LIMIT = 56 * 1024 * 1024

C_DQ, C_DK, C_DV, C_NQ, C_NKV, C_NG = 0, 512, 1024, 1536, 2048, 2816
EVEN_IN = 2840
EVEN_IN_PAD = 2944
EVEN_NORM_BLOCKS = tuple(range(0, 8)) + tuple(range(12, 16)) + (18, 20)
ODD_IN = 4096
SAMPLE_ROWS = 16


def _cparams(sem):
    return pltpu.CompilerParams(dimension_semantics=sem, vmem_limit_bytes=VMEM_LIMIT)


def _nt_dot(a, b):
    return lax.dot_general(a, b, (((1,), (1,)), ((), ())), preferred_element_type=F32)


def _rms(x, gain):
    return x * lax.rsqrt(jnp.mean(x * x, axis=-1, keepdims=True) + EPS) * gain


def _seg_rms(y, gain):
    lo = lax.broadcasted_iota(jnp.int32, y.shape, 1) < HEAD_DIM
    y2 = y * y
    s_lo = jnp.sum(jnp.where(lo, y2, 0.0), axis=-1, keepdims=True)
    s_hi = jnp.sum(jnp.where(lo, 0.0, y2), axis=-1, keepdims=True)
    ms = jnp.where(lo, s_lo, s_hi) * (1.0 / HEAD_DIM)
    return y * lax.rsqrt(ms + EPS) * gain


def _norm_proj_kernel(x_ref, g_ref, w_ref, cg_ref, o_ref, *maybe_ob_ref, norm_blocks, col_chunk):
    xn = _rms(x_ref[...], g_ref[...]).astype(BF16)
    n_cols = w_ref.shape[1]
    for c0 in range(0, n_cols, col_chunk):
        c1 = min(c0 + col_chunk, n_cols)
        y = jnp.dot(xn, w_ref[:, c0:c1], preferred_element_type=F32)
        for b in range(c0 // LANES, c1 // LANES):
            yb = y[:, b * LANES - c0:(b + 1) * LANES - c0]
            if b in norm_blocks:
                yb = _seg_rms(yb, cg_ref[:, b * LANES:(b + 1) * LANES])
            o_ref[:, b * LANES:(b + 1) * LANES] = yb
            for ob_ref in maybe_ob_ref:
                ob_ref[:, b * LANES:(b + 1) * LANES] = yb.astype(BF16)


def _norm_proj(x2d, gain, w_bf16, colgain, norm_blocks, tm, with_bf16):
    n, d = x2d.shape
    c = w_bf16.shape[1]
    kern = functools.partial(_norm_proj_kernel, norm_blocks=norm_blocks, col_chunk=512)
    n_out = 2 if with_bf16 else 1
    return pl.pallas_call(
        kern,
        grid=(n // tm,),
        in_specs=[
            pl.BlockSpec((tm, d), lambda i: (i, 0)),
            pl.BlockSpec((1, d), lambda i: (0, 0)),
            pl.BlockSpec((d, c), lambda i: (0, 0)),
            pl.BlockSpec((1, c), lambda i: (0, 0)),
        ],
        out_specs=[pl.BlockSpec((tm, c), lambda i: (i, 0))] * n_out,
        out_shape=[jax.ShapeDtypeStruct((n, c), F32), jax.ShapeDtypeStruct((n, c), BF16)][:n_out],
        compiler_params=_cparams(("parallel",)),
    )(x2d, gain.reshape(1, d), w_bf16, colgain)


def _ffn_kernel(x_ref, g_ref, wg_ref, wu_ref, wo_ref, o_ref, xn_scr, acc_scr):
    f = pl.program_id(1)

    @pl.when(f == 0)
    def _():
        xn_scr[...] = _rms(x_ref[...], g_ref[...]).astype(BF16)
        acc_scr[...] = jnp.zeros_like(acc_scr)

    xn = xn_scr[...]
    gate = jnp.dot(xn, wg_ref[...], preferred_element_type=F32)
    up = jnp.dot(xn, wu_ref[...], preferred_element_type=F32)
    act = (gate * jax.nn.sigmoid(gate) * up).astype(BF16)
    acc_scr[...] += jnp.dot(act, wo_ref[...], preferred_element_type=F32)

    @pl.when(f == pl.num_programs(1) - 1)
    def _():
        o_ref[...] = x_ref[...] + acc_scr[...]


def _ffn(x2d, gain, w_in_bf16, w_out_bf16, tm, tf):
    n, d = x2d.shape
    dff = w_out_bf16.shape[0]
    nf = dff // tf
    return pl.pallas_call(
        _ffn_kernel,
        grid=(n // tm, nf),
        in_specs=[
            pl.BlockSpec((tm, d), lambda i, f: (i, 0)),
            pl.BlockSpec((1, d), lambda i, f: (0, 0)),
            pl.BlockSpec((d, tf), lambda i, f: (0, f)),
            pl.BlockSpec((d, tf), lambda i, f: (0, nf + f)),
            pl.BlockSpec((tf, d), lambda i, f: (f, 0)),
        ],
        out_specs=pl.BlockSpec((tm, d), lambda i, f: (i, 0)),
        out_shape=jax.ShapeDtypeStruct((n, d), F32),
        scratch_shapes=[pltpu.VMEM((tm, d), BF16), pltpu.VMEM((tm, d), F32)],
        compiler_params=_cparams(("parallel", "arbitrary")),
    )(x2d, gain.reshape(1, d), w_in_bf16, w_in_bf16, w_out_bf16)


def _out_proj_kernel(a_ref, b_ref, r_ref, w_ref, o_ref):
    half = a_ref.shape[1]
    y = jnp.dot(a_ref[...].astype(BF16), w_ref[0:half, :], preferred_element_type=F32)
    y += jnp.dot(b_ref[...].astype(BF16), w_ref[half:2 * half, :], preferred_element_type=F32)
    o_ref[...] = r_ref[...] + y


def _out_proj(a, a_blk, b, b_blk, resid, w_bf16, tm):
    n, d = resid.shape
    half = w_bf16.shape[0] // 2
    return pl.pallas_call(
        _out_proj_kernel,
        grid=(n // tm,),
        in_specs=[
            pl.BlockSpec((tm, half), lambda i: (i, a_blk)),
            pl.BlockSpec((tm, half), lambda i: (i, b_blk)),
            pl.BlockSpec((tm, d), lambda i: (i, 0)),
            pl.BlockSpec((2 * half, d), lambda i: (0, 0)),
        ],
        out_specs=pl.BlockSpec((tm, d), lambda i: (i, 0)),
        out_shape=jax.ShapeDtypeStruct((n, d), F32),
        compiler_params=_cparams(("parallel",)),
    )(a, b, resid, w_bf16)


def _flash_init(m_scr, l_scr, acc_scr):
    m_scr[...] = jnp.full(m_scr.shape, NEG, F32)
    l_scr[...] = jnp.zeros(l_scr.shape, F32)
    acc_scr[...] = jnp.zeros(acc_scr.shape, F32)


def _flash_update(s, mask, v_tile, m_scr, l_scr, acc_scr):
    s = jnp.where(mask, s, NEG)
    m_old = m_scr[...]
    m_new = jnp.maximum(m_old, jnp.max(s, axis=-1, keepdims=True))
    alpha = jnp.exp(m_old - m_new)
    p = jnp.where(mask, jnp.exp(s - m_new), 0.0)
    l_scr[...] = alpha * l_scr[...] + jnp.sum(p, axis=-1, keepdims=True)
    acc_scr[...] = alpha * acc_scr[...] + jnp.dot(p.astype(BF16), v_tile, preferred_element_type=F32)
    m_scr[...] = m_new


def _diff_lambda(lp_ref, lam_init):
    lp = lp_ref[...].astype(F32)
    a = jnp.sum(lp[0:1, :] * lp[1:2, :], axis=-1, keepdims=True)
    b = jnp.sum(lp[2:3, :] * lp[3:4, :], axis=-1, keepdims=True)
    return jnp.exp(a) - jnp.exp(b) + lam_init


def _flash_init_t(m_scr, l_scr, acc_scr):
    m_scr[...] = jnp.full(m_scr.shape, M_FLOOR, F32)
    l_scr[...] = jnp.zeros(l_scr.shape, F32)
    acc_scr[...] = jnp.zeros(acc_scr.shape, F32)


def _flash_update_t(st, vt_tile, m_scr, l_scr, acc_scr):
    m_old = m_scr[...]
    m_new = jnp.maximum(m_old, jnp.max(st, axis=0, keepdims=True))
    alpha = jnp.exp2(m_old - m_new)
    p = jnp.exp2(st - m_new)
    l_scr[...] = alpha * l_scr[...] + jnp.sum(p, axis=0, keepdims=True)
    acc_scr[...] = alpha * acc_scr[...] + jnp.dot(vt_tile, p.astype(BF16), preferred_element_type=F32)
    m_scr[...] = m_new


def _diff_qstack(q, scale):
    lo = lax.broadcasted_iota(jnp.int32, q.shape, 1) < HEAD_DIM
    qs = q.astype(F32) * scale
    return jnp.concatenate([jnp.where(lo, qs, 0.0), jnp.where(lo, 0.0, qs)], axis=0)


def _diff_finish(m_scr, l_scr, acc_scr, lam, gain, lam_init, tq):
    o = acc_scr[...] / jnp.maximum(l_scr[...], 1e-30)
    d = o[0:tq] - lam * o[tq:2 * tq]
    return _rms(d, gain) * (1.0 - lam_init)


def _diff_prompt_kernel(q_ref, k_ref, vt_ref, lp_ref, sg_ref, o_ref, m_scr, l_scr, acc_scr,
                        *, tq, tk, lam_init):
    assert tq & (tq - 1) == 0
    q0 = pl.program_id(2) * tq
    qst = _diff_qstack(q_ref[...], HEAD_DIM ** -0.5 * LOG2E).T.astype(BF16)
    lane = lax.broadcasted_iota(jnp.int32, (1, 2 * tq), 1)
    qpos = q0 + (lane & (tq - 1))
    _flash_init_t(m_scr, l_scr, acc_scr)

    def body(ki, carry):
        k0 = pl.multiple_of(ki * tk, tk)
        st = jnp.dot(k_ref[pl.ds(k0, tk), :], qst, preferred_element_type=F32)
        kpos = k0 + lax.broadcasted_iota(jnp.int32, (tk, 1), 0)
        _flash_update_t(jnp.where(kpos <= qpos, st, NEG), vt_ref[:, pl.ds(k0, tk)], m_scr, l_scr, acc_scr)
        return carry

    lax.fori_loop(0, (q0 + tq - 1) // tk + 1, body, 0)
    lam = _diff_lambda(lp_ref, lam_init)
    ot = acc_scr[...] / jnp.maximum(l_scr[...], 1e-30)
    dt = ot[:, 0:tq] - lam * ot[:, tq:2 * tq]
    dn = dt * lax.rsqrt(jnp.mean(dt * dt, axis=0, keepdims=True) + EPS)
    o_ref[...] = dn.T * sg_ref[...] * (1.0 - lam_init)


def _diff_prompt(projb, vt, lam_par, subln, lam_init, tq, tk):
    b, t, _ = projb.shape
    kern = functools.partial(_diff_prompt_kernel, tq=tq, tk=tk, lam_init=lam_init)
    kblk = C_DK // LANES
    return pl.pallas_call(
        kern,
        grid=(b, DA_HEADS, t // tq),
        in_specs=[
            pl.BlockSpec((None, tq, LANES), lambda bi, h, qi: (bi, qi, h)),
            pl.BlockSpec((None, t, LANES), lambda bi, h, qi: (bi, 0, kblk + h)),
            pl.BlockSpec((None, DA_VDIM, t), lambda bi, h, qi: (bi, h, 0)),
            pl.BlockSpec((4, HEAD_DIM), lambda bi, h, qi: (0, 0)),
            pl.BlockSpec((1, DA_VDIM), lambda bi, h, qi: (0, 0)),
        ],
        out_specs=pl.BlockSpec((None, tq, LANES), lambda bi, h, qi: (bi, qi, h)),
        out_shape=jax.ShapeDtypeStruct((b, t, DA_HEADS * DA_VDIM), F32),
        scratch_shapes=[pltpu.VMEM((1, 2 * tq), F32), pltpu.VMEM((1, 2 * tq), F32),
                        pltpu.VMEM((DA_VDIM, 2 * tq), F32)],
        compiler_params=_cparams(("parallel", "parallel", "arbitrary")),
    )(projb, projb, vt, lam_par, subln.reshape(1, DA_VDIM))


def _page_copies(cache_ref, e, pt_ref, seq, page0, n_pages, lane0, n_lanes, buf, sem, page_size):
    copies = []
    for j in range(n_pages):
        page = pt_ref[seq, page0 + j]
        copies.append(pltpu.make_async_copy(
            cache_ref.at[e, page, :, pl.ds(lane0, n_lanes)],
            buf.at[pl.ds(j * page_size, page_size), :],
            sem.at[j]))
    return copies


def _diff_sample_kernel(pt_ref, q_ref, tail_ref, lp_ref, sg_ref, cache_ref, o_ref,
                        buf, sem, m_scr, l_scr, acc_scr,
                        *, e, n_chunks, pages_per_chunk, page_size, past_len, tk, lam_init):
    s_idx = pl.program_id(0)
    c_idx = pl.program_id(1)
    tq = SAMPLE_ROWS
    width = DA_HEADS * DA_VDIM
    row = lax.broadcasted_iota(jnp.int32, (2 * tq, 1), 0)
    qpos = past_len + jnp.where(row >= tq, row - tq, row)

    @pl.when(c_idx == 0)
    def _():
        _flash_init(m_scr, l_scr, acc_scr)

    def update(h, kt, vt, kpos):
        qs = _diff_qstack(q_ref[:, h * LANES:(h + 1) * LANES], HEAD_DIM ** -0.5).astype(BF16)
        hs = slice(h * 2 * tq, (h + 1) * 2 * tq)
        _flash_update(_nt_dot(qs, kt.astype(BF16)), kpos <= qpos, vt.astype(BF16),
                      m_scr.at[hs], l_scr.at[hs], acc_scr.at[hs])

    @pl.when(c_idx < n_chunks)
    def _():
        per_pos = 2 * DA_HEADS
        copies = _page_copies(cache_ref, e, pt_ref, s_idx, c_idx * pages_per_chunk, pages_per_chunk,
                              0, LANES, buf, sem, page_size * per_pos)
        for cp in copies:
            cp.start()
        for cp in copies:
            cp.wait()
        chunk_rows = pages_per_chunk * page_size
        for h in range(DA_HEADS):
            for r0 in range(0, chunk_rows, tk):
                kpos = (c_idx * chunk_rows + r0) + lax.broadcasted_iota(jnp.int32, (1, tk), 1)
                update(h, buf[pl.ds(r0 * per_pos + h, tk, stride=per_pos), :],
                       buf[pl.ds(r0 * per_pos + DA_HEADS + h, tk, stride=per_pos), :], kpos)

    @pl.when(c_idx == n_chunks)
    def _():
        n_tail = tail_ref.shape[0]
        kpos = past_len + lax.broadcasted_iota(jnp.int32, (1, n_tail), 1)
        lam = _diff_lambda(lp_ref, lam_init)
        for h in range(DA_HEADS):
            update(h, tail_ref[:, h * LANES:(h + 1) * LANES],
                   tail_ref[:, width + h * LANES:width + (h + 1) * LANES], kpos)
            hs = slice(h * 2 * tq, (h + 1) * 2 * tq)
            o_ref[:, h * LANES:(h + 1) * LANES] = _diff_finish(
                m_scr.at[hs], l_scr.at[hs], acc_scr.at[hs], lam, sg_ref[...], lam_init, tq)


def _diff_sample(projb_s, tail, cache4, e, page_table, lam_par, subln, lam_init, past_len):
    s, tq, _ = projb_s.shape
    n_pages = page_table.shape[1]
    page_size = cache4.shape[2] // (2 * DA_HEADS)
    pages_per_chunk = math.gcd(n_pages, 16)
    n_chunks = n_pages // pages_per_chunk
    width = DA_HEADS * DA_VDIM
    tk = math.gcd(pages_per_chunk * page_size, 512)
    kern = functools.partial(_diff_sample_kernel, e=e, n_chunks=n_chunks, pages_per_chunk=pages_per_chunk,
                             page_size=page_size, past_len=past_len, tk=tk, lam_init=lam_init)
    grid_spec = pltpu.PrefetchScalarGridSpec(
        num_scalar_prefetch=1,
        grid=(s, n_chunks + 1),
        in_specs=[
            pl.BlockSpec((None, tq, width), lambda si, ci, pt: (si, 0, 0)),
            pl.BlockSpec((None, tail.shape[1], 2 * width), lambda si, ci, pt: (si, 0, 0)),
            pl.BlockSpec((4, HEAD_DIM), lambda si, ci, pt: (0, 0)),
            pl.BlockSpec((1, DA_VDIM), lambda si, ci, pt: (0, 0)),
            pl.BlockSpec(memory_space=pl.ANY),
        ],
        out_specs=pl.BlockSpec((None, tq, width), lambda si, ci, pt: (si, 0, 0)),
        scratch_shapes=[
            pltpu.VMEM((pages_per_chunk * page_size * 2 * DA_HEADS, LANES), F32),
            pltpu.SemaphoreType.DMA((pages_per_chunk,)),
            pltpu.VMEM((DA_HEADS * 2 * tq, 1), F32),
            pltpu.VMEM((DA_HEADS * 2 * tq, 1), F32),
            pltpu.VMEM((DA_HEADS * 2 * tq, LANES), F32),
        ],
    )
    return pl.pallas_call(
        kern,
        grid_spec=grid_spec,
        out_shape=jax.ShapeDtypeStruct((s, tq, width), F32),
        compiler_params=_cparams(("arbitrary", "arbitrary")),
    )(page_table, projb_s, tail, lam_par, subln.reshape(1, DA_VDIM), cache4)


def _compress_compute(xk_ref, xv_ref, pe_ref, w1_ref, w2_ref, kg_ref, kcb_ref, vcb_ref, nb):
    def body(l, accs):
        pe = pe_ref[pl.ds(l, 1), :]
        ak = (xk_ref[pl.ds(l, nb, stride=NSA_BLOCK), :] + pe[:, 0:LANES]).astype(BF16)
        av = (xv_ref[pl.ds(l, nb, stride=NSA_BLOCK), :] + pe[:, LANES:2 * LANES]).astype(BF16)
        hk = jnp.dot(ak, w1_ref[0, l], preferred_element_type=F32)
        hv = jnp.dot(av, w1_ref[1, l], preferred_element_type=F32)
        return accs[0] + hk, accs[1] + hv

    zero = jnp.zeros((nb, 2 * NSA_CMP_HID), F32)
    hk, hv = lax.fori_loop(0, NSA_BLOCK, body, (zero, zero))
    ck = jnp.dot(jax.nn.gelu(hk).astype(BF16), w2_ref[0], preferred_element_type=F32)
    cv = jnp.dot(jax.nn.gelu(hv).astype(BF16), w2_ref[1], preferred_element_type=F32)
    ck = _seg_rms(ck, kg_ref[...])
    for ref in (kcb_ref, vcb_ref):
        ref[...] = jnp.zeros(ref.shape, ref.dtype)
    kcb_ref[0:nb, :] = ck.astype(BF16)
    vcb_ref[0:nb, :] = cv.astype(BF16)


def _compress_prompt_kernel(xk_ref, xv_ref, pe_ref, w1_ref, w2_ref, kg_ref, kcb_ref, vcb_ref, *, nb):
    _compress_compute(xk_ref, xv_ref, pe_ref, w1_ref, w2_ref, kg_ref, kcb_ref, vcb_ref, nb)


def _compress_specs(nbp):
    hid2 = 2 * NSA_CMP_HID
    w_specs = [
        pl.BlockSpec((NSA_BLOCK, 2 * LANES), lambda *a: (0, 0)),
        pl.BlockSpec((2, NSA_BLOCK, LANES, hid2), lambda *a: (0, 0, 0, 0)),
        pl.BlockSpec((2, hid2, LANES), lambda *a: (0, 0, 0)),
        pl.BlockSpec((1, LANES), lambda *a: (0, 0)),
    ]
    out_specs = [pl.BlockSpec((None, nbp, LANES), lambda bi, *a: (bi, 0, 0))] * 2
    return w_specs, out_specs


def _compress_out_shape(nbat, nbp):
    return [jax.ShapeDtypeStruct((nbat, nbp, LANES), BF16)] * 2


def _compress_prompt(projf, cw):
    b, t, _ = projf.shape
    nb = t // NSA_BLOCK
    nbp = -(-nb // LANES) * LANES
    w_specs, out_specs = _compress_specs(nbp)
    return pl.pallas_call(
        functools.partial(_compress_prompt_kernel, nb=nb),
        grid=(b,),
        in_specs=[pl.BlockSpec((None, t, LANES), lambda bi: (bi, 0, C_NKV // LANES)),
                  pl.BlockSpec((None, t, LANES), lambda bi: (bi, 0, C_NKV // LANES + 1))] + w_specs,
        out_specs=out_specs,
        out_shape=_compress_out_shape(b, nbp),
        compiler_params=_cparams(("parallel",)),
    )(projf, projf, *cw)


def _compress_sample_kernel(pt_ref, tail_ref, pe_ref, w1_ref, w2_ref, kg_ref, cache_ref,
                            kcb_ref, vcb_ref, kbuf, vbuf, ksem, vsem, *, e, n_pages, page_size, nb):
    s_idx = pl.program_id(0)
    copies = (_page_copies(cache_ref, e, pt_ref, s_idx, 0, n_pages, 0, LANES, kbuf, ksem, page_size)
              + _page_copies(cache_ref, e, pt_ref, s_idx, 0, n_pages, LANES, LANES, vbuf, vsem, page_size))
    for cp in copies:
        cp.start()
    kbuf[n_pages * page_size:, :] = tail_ref[:, 0:LANES]
    vbuf[n_pages * page_size:, :] = tail_ref[:, LANES:2 * LANES]
    for cp in copies:
        cp.wait()
    _compress_compute(kbuf, vbuf, pe_ref, w1_ref, w2_ref, kg_ref, kcb_ref, vcb_ref, nb)


def _compress_sample(tail, cache3, e, page_table, cw):
    s = tail.shape[0]
    n_pages = page_table.shape[1]
    page_size = cache3.shape[2]
    rows = n_pages * page_size + tail.shape[1]
    nb = rows // NSA_BLOCK
    nbp = -(-nb // LANES) * LANES
    w_specs, out_specs = _compress_specs(nbp)
    grid_spec = pltpu.PrefetchScalarGridSpec(
        num_scalar_prefetch=1,
        grid=(s,),
        in_specs=[pl.BlockSpec((None, tail.shape[1], 2 * LANES), lambda si, pt: (si, 0, 0))] + w_specs
        + [pl.BlockSpec(memory_space=pl.ANY)],
        out_specs=out_specs,
        scratch_shapes=[pltpu.VMEM((rows, LANES), F32), pltpu.VMEM((rows, LANES), F32),
                        pltpu.SemaphoreType.DMA((n_pages,)), pltpu.SemaphoreType.DMA((n_pages,))],
    )
    return pl.pallas_call(
        functools.partial(_compress_sample_kernel, e=e, n_pages=n_pages, page_size=page_size, nb=nb),
        grid_spec=grid_spec,
        out_shape=_compress_out_shape(s, nbp),
        compiler_params=_cparams(("arbitrary",)),
    )(page_table, tail, *cw, cache3)


def _nsa_branches(q, kc_ref, vc_ref, k_sel_tile, vt_sel_tile, k_win_tile, vt_win_tile,
                  sel_scr, m_s, l_s, a_s, m_w, l_w, a_w,
                  *, tq, q0, n_blk, n_sel_rows, tk, wbase, n_win_rows, tkw):
    nh = NSA_HEADS
    n_rows = nh * tq
    blk_shift = NSA_BLOCK.bit_length() - 1
    tq_shift = tq.bit_length() - 1
    assert 1 << tq_shift == tq and 1 << blk_shift == NSA_BLOCK
    lane128 = lax.broadcasted_iota(jnp.int32, (tq, LANES), 1)
    qf = q.astype(F32) * (HEAD_DIM ** -0.5 * LOG2E)
    parts = []
    for hh in range(nh):
        g = hh // NSA_REP
        blk = qf[:, (hh // 2) * LANES:(hh // 2 + 1) * LANES]
        if hh % 2 != g:
            blk = pltpu.roll(blk, HEAD_DIM, 1)
        in_seg = (lane128 >= g * HEAD_DIM) & (lane128 < (g + 1) * HEAD_DIM)
        parts.append(jnp.where(in_seg, blk, 0.0))
    qst = jnp.concatenate(parts, axis=0).T.astype(BF16)

    lane = lax.broadcasted_iota(jnp.int32, (1, n_rows), 1)
    tok = lane & (tq - 1)
    qpos = q0 + tok

    nbp = kc_ref.shape[0]
    nb8 = sel_scr.shape[0]
    assert n_blk <= nb8 <= nbp and n_sel_rows // NSA_BLOCK <= nb8
    sc = jnp.dot(kc_ref[...], qst, preferred_element_type=F32)
    blk_p = lax.broadcasted_iota(jnp.int32, (nbp, 1), 0)
    cmp_ok = blk_p < ((qpos + 1) >> blk_shift)
    sc = jnp.where(cmp_ok, sc, NEG)
    mc = jnp.max(sc, axis=0, keepdims=True)
    ec = jnp.where(cmp_ok, jnp.exp2(sc - mc), 0.0)
    pc_all = ec / jnp.maximum(jnp.sum(ec, axis=0, keepdims=True), 1e-30)
    vct = vc_ref[...].astype(F32).T.astype(BF16)
    o_cmp = jnp.dot(vct, pc_all.astype(BF16), preferred_element_type=F32)
    pc = pc_all[0:nb8]
    blk_i = lax.broadcasted_iota(jnp.int32, (nb8, 1), 0)
    blk_f = blk_i.astype(F32)

    n_pick = min(NSA_TOPK - 1, n_blk)

    def pick(imp, cur):
        cand = blk_i < cur
        x = jnp.where(cand, imp, -1.0)
        x = jnp.where(blk_i < n_blk, x, -2.0)
        picked = jnp.zeros(x.shape, F32)
        for _ in range(n_pick):
            mx = jnp.max(x, axis=0, keepdims=True)
            first = jnp.min(jnp.where(x == mx, blk_f, float(nb8)), axis=0, keepdims=True)
            hit = blk_f == first
            picked = jnp.where(hit, 1.0, picked)
            x = jnp.where(hit, -3.0, x)
        return jnp.where(blk_i == cur, 1.0, jnp.where(cand, picked, 0.0))

    if tq % LANES == 0:
        cur_t = (q0 + lax.broadcasted_iota(jnp.int32, (1, tq), 1)) >> blk_shift
        for g in range(NSA_GROUPS):
            base = g * NSA_REP * tq
            imp = pc[:, base:base + tq]
            for r in range(1, NSA_REP):
                imp = imp + pc[:, base + r * tq:base + (r + 1) * tq]
            sel_g = pick(imp, cur_t)
            for r in range(NSA_REP):
                sel_scr[:, base + r * tq:base + (r + 1) * tq] = sel_g
    else:
        assert n_rows == LANES
        rep = (lane >> tq_shift) & (NSA_REP - 1)
        imp = pc
        for d in range(1, NSA_REP):
            up = pltpu.roll(pc, LANES - d * tq, 1)
            dn = pltpu.roll(pc, d * tq, 1)
            imp = imp + jnp.where(rep + d < NSA_REP, up, 0.0) + jnp.where(rep >= d, dn, 0.0)
        sel_scr[...] = pick(imp, qpos >> blk_shift)

    _flash_init_t(m_s, l_s, a_s)
    blocks_per_tile = tk // NSA_BLOCK

    def sel_body(ki, carry):
        k0 = pl.multiple_of(ki * tk, tk)
        st = jnp.dot(k_sel_tile(k0), qst, preferred_element_type=F32)
        b0 = ki * blocks_per_tile
        kpos = k0 + lax.broadcasted_iota(jnp.int32, (tk, 1), 0)
        pieces = []
        for bb in range(blocks_per_tile):
            rs = slice(bb * NSA_BLOCK, (bb + 1) * NSA_BLOCK)
            ok = (sel_scr[pl.ds(b0 + bb, 1), :] > 0.5) & (kpos[rs] <= qpos)
            pieces.append(jnp.where(ok, st[rs], NEG))
        _flash_update_t(jnp.concatenate(pieces, axis=0), vt_sel_tile(k0), m_s, l_s, a_s)
        return carry

    last_sel = jnp.minimum((q0 + tq - 1) // tk, n_sel_rows // tk - 1)
    lax.fori_loop(0, last_sel + 1, sel_body, 0)

    _flash_init_t(m_w, l_w, a_w)

    def win_body(ki, carry):
        k0 = pl.multiple_of(ki * tkw, tkw)
        st = jnp.dot(k_win_tile(k0), qst, preferred_element_type=F32)
        kpos = wbase + k0 + lax.broadcasted_iota(jnp.int32, (tkw, 1), 0)
        ok = (kpos <= qpos) & (kpos >= qpos - NSA_WINDOW) & (kpos >= 0)
        _flash_update_t(jnp.where(ok, st, NEG), vt_win_tile(k0), m_w, l_w, a_w)
        return carry

    lo_tile = jnp.maximum(q0 - NSA_WINDOW - wbase, 0) // tkw
    hi_tile = jnp.minimum((q0 + tq - 1 - wbase) // tkw, n_win_rows // tkw - 1)
    lax.fori_loop(lo_tile, hi_tile + 1, win_body, 0)

    o_sel = a_s[...] / jnp.maximum(l_s[...], 1e-30)
    o_win = a_w[...] / jnp.maximum(l_w[...], 1e-30)
    return o_cmp, o_sel, o_win


def _nsa_gate_lanes(o_cmp, o_sel, o_win, gate_logits, tq):
    gt = jax.nn.sigmoid(gate_logits).T
    def gate_row(branch):
        return jnp.concatenate([gt[3 * hh + branch:3 * hh + branch + 1, :] for hh in range(NSA_HEADS)], axis=1)
    ot = gate_row(0) * o_cmp + gate_row(1) * o_sel + gate_row(2) * o_win
    outs = []
    for j in range(NSA_HEADS // 2):
        g = (2 * j) // NSA_REP
        rs = slice(g * HEAD_DIM, (g + 1) * HEAD_DIM)
        pair = jnp.concatenate([ot[rs, (2 * j) * tq:(2 * j + 1) * tq],
                                ot[rs, (2 * j + 1) * tq:(2 * j + 2) * tq]], axis=0)
        outs.append(pair.T)
    return jnp.concatenate(outs, axis=1)


def _nsa_gate_rows(o_cmp, o_sel, o_win, gate_logits, tq):
    oc, os_, ow = o_cmp.T, o_sel.T, o_win.T
    gates = jax.nn.sigmoid(gate_logits)
    lo_half = lax.broadcasted_iota(jnp.int32, (tq, LANES), 1) < HEAD_DIM
    heads = []
    for hh in range(NSA_HEADS):
        g = hh // NSA_REP
        rs = slice(hh * tq, (hh + 1) * tq)
        o = (gates[:, 3 * hh:3 * hh + 1] * oc[rs] + gates[:, 3 * hh + 1:3 * hh + 2] * os_[rs]
             + gates[:, 3 * hh + 2:3 * hh + 3] * ow[rs])
        if hh % 2 != g:
            o = pltpu.roll(o, HEAD_DIM, 1)
        heads.append(o)
    return jnp.concatenate([jnp.where(lo_half, heads[2 * j], heads[2 * j + 1])
                            for j in range(NSA_HEADS // 2)], axis=1)


def _nsa_scratch_t(tq, nb8):
    n_rows = NSA_HEADS * tq
    one = [pltpu.VMEM((1, n_rows), F32), pltpu.VMEM((1, n_rows), F32), pltpu.VMEM((LANES, n_rows), F32)]
    return [pltpu.VMEM((nb8, n_rows), F32)] + one + one


def _nsa_prompt_kernel_t(q_ref, g_ref, kc_ref, vc_ref, ks_ref, vst_ref, kw_ref, vwt_ref, o_ref,
                         sel_scr, m_s, l_s, a_s, m_w, l_w, a_w, *, tq, tk, tkw, n_blk, t):
    q0 = pl.program_id(1) * tq
    branches = _nsa_branches(
        q_ref[...], kc_ref, vc_ref,
        lambda k0: ks_ref[pl.ds(k0, tk), :], lambda k0: vst_ref[:, pl.ds(k0, tk)],
        lambda k0: kw_ref[pl.ds(k0, tkw), :], lambda k0: vwt_ref[:, pl.ds(k0, tkw)],
        sel_scr, m_s, l_s, a_s, m_w, l_w, a_w,
        tq=tq, q0=q0, n_blk=n_blk, n_sel_rows=t, tk=tk, wbase=0, n_win_rows=t, tkw=tkw)
    o_ref[...] = _nsa_gate_lanes(*branches, g_ref[...], tq)


def _nsa_prompt_t(projf, projb, vst, vwt, kcb, vcb, tq, tk):
    b, t, _ = projb.shape
    nbp = kcb.shape[1]
    n_blk = t // NSA_BLOCK
    nb8 = -(-n_blk // (2 * SUBLANES)) * (2 * SUBLANES)
    tkw = min(tq, LANES)
    kern = functools.partial(_nsa_prompt_kernel_t, tq=tq, tk=tk, tkw=tkw, n_blk=n_blk, t=t)
    ks_blk = (C_NKV + 2 * LANES) // LANES
    kw_blk = (C_NKV + 4 * LANES) // LANES
    return pl.pallas_call(
        kern,
        grid=(b, t // tq),
        in_specs=[
            pl.BlockSpec((None, tq, 512), lambda bi, qi: (bi, qi, C_NQ // 512)),
            pl.BlockSpec((None, tq, LANES), lambda bi, qi: (bi, qi, C_NG // LANES)),
            pl.BlockSpec((None, nbp, LANES), lambda bi, qi: (bi, 0, 0)),
            pl.BlockSpec((None, nbp, LANES), lambda bi, qi: (bi, 0, 0)),
            pl.BlockSpec((None, t, LANES), lambda bi, qi: (bi, 0, ks_blk)),
            pl.BlockSpec((None, LANES, t), lambda bi, qi: (bi, 0, 0)),
            pl.BlockSpec((None, t, LANES), lambda bi, qi: (bi, 0, kw_blk)),
            pl.BlockSpec((None, LANES, t), lambda bi, qi: (bi, 0, 0)),
        ],
        out_specs=pl.BlockSpec((None, tq, 512), lambda bi, qi: (bi, qi, 0)),
        out_shape=jax.ShapeDtypeStruct((b, t, NSA_HEADS * HEAD_DIM), F32),
        scratch_shapes=_nsa_scratch_t(tq, nb8),
        compiler_params=_cparams(("parallel", "arbitrary")),
    )(projb, projf, kcb, vcb, projb, vst, projb, vwt)


def _nsa_sample_kernel_t(pt_ref, q_ref, g_ref, kc_ref, vc_ref, tail_ref, kwv_ref, cache_ref, o_ref,
                         buf, sem, sel_scr, m_s, l_s, a_s, m_w, l_w, a_w,
                         *, e, n_pages, page_size, past_len, n_blk, tk, tkw, wbase):
    s_idx = pl.program_id(0)
    copies = _page_copies(cache_ref, e, pt_ref, s_idx, 0, n_pages, 2 * LANES, 2 * LANES, buf, sem, page_size)
    for cp in copies:
        cp.start()
    buf[n_pages * page_size:, :] = tail_ref[...]
    for cp in copies:
        cp.wait()
    tq = q_ref.shape[0]
    branches = _nsa_branches(
        q_ref[...], kc_ref, vc_ref,
        lambda k0: buf[pl.ds(k0, tk), 0:LANES].astype(BF16),
        lambda k0: buf[pl.ds(k0, tk), LANES:2 * LANES].T.astype(BF16),
        lambda k0: kwv_ref[pl.ds(k0, tkw), 0:LANES].astype(BF16),
        lambda k0: kwv_ref[pl.ds(k0, tkw), LANES:2 * LANES].T.astype(BF16),
        sel_scr, m_s, l_s, a_s, m_w, l_w, a_w,
        tq=tq, q0=past_len, n_blk=n_blk, n_sel_rows=buf.shape[0], tk=tk,
        wbase=wbase, n_win_rows=kwv_ref.shape[0], tkw=tkw)
    o_ref[...] = _nsa_gate_rows(*branches, g_ref[...], tq)


def _nsa_sample_t(projf_s, projb_s, kcb, vcb, sel_tail, kwv, cache3, e, page_table, past_len, n_valid):
    s, tq, _ = projb_s.shape
    n_pages = page_table.shape[1]
    page_size = cache3.shape[2]
    rows = n_pages * page_size + sel_tail.shape[1]
    nbp = kcb.shape[1]
    n_blk = -(-(past_len + n_valid) // NSA_BLOCK)
    nb8 = -(-n_blk // (2 * SUBLANES)) * (2 * SUBLANES)
    tk = math.gcd(rows, LANES)
    tkw = math.gcd(kwv.shape[1], LANES)
    wbase = past_len - (kwv.shape[1] - sel_tail.shape[1])
    kern = functools.partial(_nsa_sample_kernel_t, e=e, n_pages=n_pages, page_size=page_size,
                             past_len=past_len, n_blk=n_blk, tk=tk, tkw=tkw, wbase=wbase)
    grid_spec = pltpu.PrefetchScalarGridSpec(
        num_scalar_prefetch=1,
        grid=(s,),
        in_specs=[
            pl.BlockSpec((None, tq, 512), lambda si, pt: (si, 0, C_NQ // 512)),
            pl.BlockSpec((None, tq, LANES), lambda si, pt: (si, 0, C_NG // LANES)),
            pl.BlockSpec((None, nbp, LANES), lambda si, pt: (si, 0, 0)),
            pl.BlockSpec((None, nbp, LANES), lambda si, pt: (si, 0, 0)),
            pl.BlockSpec((None, sel_tail.shape[1], 2 * LANES), lambda si, pt: (si, 0, 0)),
            pl.BlockSpec((None, kwv.shape[1], 2 * LANES), lambda si, pt: (si, 0, 0)),
            pl.BlockSpec(memory_space=pl.ANY),
        ],
        out_specs=pl.BlockSpec((None, tq, 512), lambda si, pt: (si, 0, 0)),
        scratch_shapes=[pltpu.VMEM((rows, 2 * LANES), F32), pltpu.SemaphoreType.DMA((n_pages,))]
        + _nsa_scratch_t(tq, nb8),
    )
    return pl.pallas_call(
        kern,
        grid_spec=grid_spec,
        out_shape=jax.ShapeDtypeStruct((s, tq, NSA_HEADS * HEAD_DIM), F32),
        compiler_params=_cparams(("arbitrary",)),
    )(page_table, projb_s, projf_s, kcb, vcb, sel_tail, kwv, cache3)


def _split3(x):
    hi = x.astype(BF16)
    r1 = x - hi.astype(F32)
    mid = r1.astype(BF16)
    lo = (r1 - mid.astype(F32)).astype(BF16)
    return hi, mid, lo


def _hgrn_kernel(q_ref, f_ref, i_ref, g_ref, lb_ref, ng_ref, s0_ref, esel_ref, o_ref, sfin_ref,
                 st_scr, cum_scr, k_scr, *, chunk, n_chunks, n_valid):
    ti = pl.program_id(2)

    @pl.when(ti == 0)
    def _():
        st_scr[...] = s0_ref[...]

    lb = lb_ref[...]
    tri = (lax.broadcasted_iota(jnp.int32, (chunk, chunk), 0)
           >= lax.broadcasted_iota(jnp.int32, (chunk, chunk), 1)).astype(BF16)
    row = lax.broadcasted_iota(jnp.int32, (chunk, 1), 0)
    row8 = lax.broadcasted_iota(jnp.int32, (SUBLANES, 1), 0)
    n_sub = chunk // SUBLANES
    zero8 = jnp.zeros((SUBLANES, HG_DK), F32)

    def chunk_body(c, carry):
        r0 = pl.multiple_of(c * chunk, chunk)
        qr = q_ref[pl.ds(r0, chunk), :]
        q = qr * jax.nn.sigmoid(qr) * (HG_DK ** -0.5)
        fg = lb + (1.0 - lb) * jax.nn.sigmoid(f_ref[pl.ds(r0, chunk), :])
        logf = jnp.log(fg)
        k = 1.0 - fg
        v = i_ref[pl.ds(r0, chunk), :]
        if n_valid is not None:
            live = (ti * (chunk * n_chunks) + r0 + row) < n_valid
            logf = jnp.where(live, logf, 0.0)
            k = jnp.where(live, k, 0.0)
        cum = sum(jnp.dot(tri, part, preferred_element_type=F32) for part in _split3(logf))
        cum2 = cum * LOG2E
        cum_scr[...] = cum2
        k_scr[...] = k
        st = st_scr[...]
        inter = _nt_dot((q * jnp.exp2(cum2)).astype(BF16), st.astype(BF16))

        q_sub = [q[i * SUBLANES:(i + 1) * SUBLANES] for i in range(n_sub)]
        c_sub = [cum2[i * SUBLANES:(i + 1) * SUBLANES] for i in range(n_sub)]
        cols = []
        for s in range(chunk):
            j = s // SUBLANES
            cs = jnp.broadcast_to(cum_scr[s:s + 1, :], (SUBLANES, HG_DK))
            ks = jnp.broadcast_to(k_scr[s:s + 1, :], (SUBLANES, HG_DK))
            parts = []
            for i in range(n_sub):
                if i < j:
                    parts.append(zero8)
                    continue
                dec = jnp.exp2(c_sub[i] - cs)
                if i == j:
                    dec = jnp.where(row8 >= s - j * SUBLANES, dec, 0.0)
                parts.append(q_sub[i] * ks * dec)
            cols.append(jnp.concatenate(parts, axis=0).astype(BF16))
        att = jnp.dot(jnp.concatenate(cols, axis=1), esel_ref[...], preferred_element_type=F32)
        o = inter + jnp.dot(att[:, 0:chunk].astype(BF16), v.astype(BF16), preferred_element_type=F32)

        last = cum2[chunk - 1:chunk, :]
        kd = k * jnp.exp2(last - cum2)
        st_scr[...] = st * jnp.exp2(last) + jnp.dot(v.T.astype(BF16), kd.astype(BF16),
                                                     preferred_element_type=F32)
        gr = g_ref[pl.ds(r0, chunk), :]
        o_ref[pl.ds(r0, chunk), :] = _rms(o, ng_ref[...]) * (gr * jax.nn.sigmoid(gr))
        return carry

    lax.fori_loop(0, n_chunks, chunk_body, 0)

    @pl.when(ti == pl.num_programs(2) - 1)
    def _():
        sfin_ref[...] = st_scr[...]


def _hgrn(projf, lb, norm_gain, s0t, chunk, tt, n_valid):
    b, t, _ = projf.shape
    dk = HG_DK
    kern = functools.partial(_hgrn_kernel, chunk=chunk, n_chunks=tt // chunk, n_valid=n_valid)
    col = lambda j: pl.BlockSpec((None, tt, dk), lambda bi, h, ti: (bi, ti, j * HG_HEADS + h))
    vec = pl.BlockSpec((1, dk), lambda bi, h, ti: (0, h))
    st_spec = pl.BlockSpec((None, None, dk, dk), lambda bi, h, ti: (bi, h, 0, 0))
    esel = (jnp.arange(chunk * dk)[:, None] // dk == jnp.arange(LANES)[None, :]).astype(BF16)
    return pl.pallas_call(
        kern,
        grid=(b, HG_HEADS, t // tt),
        in_specs=[col(0), col(1), col(2), col(3), vec, vec, st_spec,
                  pl.BlockSpec((chunk * dk, LANES), lambda bi, h, ti: (0, 0))],
        out_specs=[pl.BlockSpec((None, tt, dk), lambda bi, h, ti: (bi, ti, h)), st_spec],
        out_shape=[jax.ShapeDtypeStruct((b, t, HG_HEADS * dk), F32),
                   jax.ShapeDtypeStruct((b, HG_HEADS, dk, dk), F32)],
        scratch_shapes=[pltpu.VMEM((dk, dk), F32)] + [pltpu.VMEM((chunk, dk), F32)] * 2,
        compiler_params=_cparams(("parallel", "parallel", "arbitrary")),
    )(projf, projf, projf, projf, lb.reshape(1, -1), norm_gain.reshape(1, -1), s0t, esel)


def _row_tile(n, pref):
    return math.gcd(n, pref)


def _pad_rows(x, rows):
    return jnp.pad(x, ((0, 0), (0, rows - x.shape[1]), (0, 0)))


def _even_weights(w_in, qk_a, qk_b, pe, w1, w2):
    w_pad = jnp.pad(w_in, ((0, 0), (0, EVEN_IN_PAD - EVEN_IN))).astype(BF16)
    ones = jnp.ones((HEAD_DIM,), F32)
    segs = [qk_a[0]] * 8 + [qk_a[1]] * 8 + [ones] * 8 + [qk_b[0]] * 8 + [ones] * 4 + [qk_b[2]] * 2 \
        + [ones] * 2 + [qk_b[3]] * 2 + [ones] * 4
    colgain = jnp.concatenate(segs).reshape(1, EVEN_IN_PAD).astype(F32)
    pe_cat = jnp.concatenate([pe[0], pe[0], pe[1], pe[1]], axis=-1).astype(F32)
    z1 = jnp.zeros_like(w1)
    w1bd = jnp.concatenate([jnp.concatenate([w1, z1], axis=-1), jnp.concatenate([z1, w1], axis=-1)],
                           axis=-2).astype(BF16)
    z2 = jnp.zeros_like(w2)
    w2bd = jnp.concatenate([jnp.concatenate([w2, z2], axis=-1), jnp.concatenate([z2, w2], axis=-1)],
                           axis=-2).astype(BF16)
    kgain = jnp.concatenate([qk_b[1], qk_b[1]]).reshape(1, LANES).astype(F32)
    return w_pad, colgain, (pe_cat, w1bd, w2bd, kgain)


def kernel(x_prompt, x_sample, cache_diff_kv, cache_nsa_kv, cache_nsa_win, state_hgrn, page_table,
           norm_mix, norm_ffn, w_ffn_in, w_ffn_out, w_in_even, w_out_even, diff_qk_gain, diff_lambda,
           diff_subln_gain, nsa_qk_gain, nsa_cmp_pe, nsa_cmp_w1, nsa_cmp_w2, w_in_odd, w_out_odd,
           hgrn_norm_gain, hgrn_lb_logits):
    b, t, d = x_prompt.shape
    s, ts, _ = x_sample.shape
    depth = norm_mix.shape[0]
    n_even = cache_diff_kv.shape[0]
    n_pool, page_size = cache_diff_kv.shape[1], cache_diff_kv.shape[2]
    past_len = page_table.shape[1] * page_size
    n_buf = cache_nsa_win.shape[2]
    tsp = SAMPLE_ROWS
    assert ts <= tsp and d == D_MODEL

    lbw = jax.nn.softmax(hgrn_lb_logits.astype(F32), axis=0)
    lower_bounds = jnp.cumsum(lbw, axis=0) - lbw[0]

    hp = x_prompt.reshape(b * t, d)
    hs = _pad_rows(x_sample, tsp).reshape(s * tsp, d)
    tm_p = _row_tile(b * t, 512)
    tm_s = _row_tile(s * tsp, 256)
    cache_diff4 = cache_diff_kv.reshape(n_even, n_pool, page_size * 2 * DA_HEADS, DA_VDIM)
    cache_nsa3 = cache_nsa_kv.reshape(n_even, n_pool, page_size, 4 * NSA_GROUPS * HEAD_DIM)
    cache_win = cache_nsa_win.reshape(n_even, s, n_buf, 2 * NSA_GROUPS * HEAD_DIM)
    live = (jnp.arange(tsp) < ts)[None, :, None]

    dkv_p, dkv_s, nkv_p, nkv_s, win_p, win_s, hg_p, hg_s = [], [], [], [], [], [], [], []
    for l in range(depth):
        if l % 2 == 0:
            e = l // 2
            lam_init = 0.8 - 0.6 * math.exp(-0.3 * l)
            w_pad, colgain, cw = _even_weights(w_in_even[e], diff_qk_gain[e], nsa_qk_gain[e],
                                               nsa_cmp_pe[e], nsa_cmp_w1[e], nsa_cmp_w2[e])
            w_out = w_out_even[e].astype(BF16)
            pf, pb = _norm_proj(hp, norm_mix[l], w_pad, colgain, EVEN_NORM_BLOCKS, tm_p, True)
            pf3, pb3 = pf.reshape(b, t, -1), pb.reshape(b, t, -1)
            feat_major = lambda c0, c1: jnp.swapaxes(pb3[:, :, c0:c1], 1, 2)
            da = _diff_prompt(pb3, feat_major(C_DV, C_NQ), diff_lambda[e], diff_subln_gain[e], lam_init,
                              tq=_row_tile(t, 256), tk=_row_tile(t, 512))
            kcb, vcb = _compress_prompt(pf3, cw)
            onsa = _nsa_prompt_t(pf3, pb3, feat_major(C_NKV + 3 * LANES, C_NKV + 4 * LANES),
                                 feat_major(C_NKV + 5 * LANES, C_NKV + 6 * LANES), kcb, vcb,
                                 tq=_row_tile(t, 128), tk=_row_tile(t, 256))
            hp = _out_proj(da.reshape(b * t, -1), 0, onsa.reshape(b * t, -1), 0, hp, w_out, tm_p)
            dkv_p.append(pf3[:, :, C_DK:C_NQ].reshape(b, t, 2, DA_HEADS, DA_VDIM))
            nkv_p.append(pf3[:, :, C_NKV:C_NKV + 512].reshape(b, t, 4, NSA_GROUPS, HEAD_DIM))
            nw = min(NSA_WINDOW, t)
            win_p.append(pf3[:, t - nw:, C_NKV + 512:C_NG].reshape(b, nw, 2, NSA_GROUPS, HEAD_DIM))
            sf, sb = _norm_proj(hs, norm_mix[l], w_pad, colgain, EVEN_NORM_BLOCKS, tm_s, True)
            sf3, sb3 = sf.reshape(s, tsp, -1), sb.reshape(s, tsp, -1)
            diff_tail = _pad_rows(sf3[:, :, C_DK:C_NQ], LANES)
            da_s = _diff_sample(sb3, diff_tail, cache_diff4, e, page_table, diff_lambda[e],
                                diff_subln_gain[e], lam_init, past_len)
            cmp_tail = _pad_rows(jnp.where(live, sf3[:, :, C_NKV:C_NKV + 256], 0.0), LANES)
            kcb_s, vcb_s = _compress_sample(cmp_tail, cache_nsa3, e, page_table, cw)
            sel_tail = _pad_rows(sf3[:, :, C_NKV + 256:C_NKV + 512], LANES)
            new_win = sf3[:, :, C_NKV + 512:C_NG]
            kwv = jnp.concatenate([cache_win[e], _pad_rows(new_win, LANES)], axis=1)
            onsa_s = _nsa_sample_t(sf3, sb3, kcb_s, vcb_s, sel_tail, kwv, cache_nsa3, e, page_table,
                                   past_len, ts)
            hs = _out_proj(da_s.reshape(s * tsp, -1), 0, onsa_s.reshape(s * tsp, -1), 0, hs, w_out, tm_s)
            dkv_s.append(sf3[:, :ts, C_DK:C_NQ].reshape(s, ts, 2, DA_HEADS, DA_VDIM))
            nkv_s.append(sf3[:, :ts, C_NKV:C_NKV + 512].reshape(s, ts, 4, NSA_GROUPS, HEAD_DIM))
            win_all = jnp.concatenate([cache_win[e], new_win[:, :ts]], axis=1)[:, -n_buf:]
            win_s.append(win_all.reshape(s, n_buf, 2, NSA_GROUPS, HEAD_DIM))
        else:
            r = l // 2
            w_in = w_in_odd[r].astype(BF16)
            w_out = w_out_odd[r].astype(BF16)
            ones = jnp.ones((1, ODD_IN), F32)
            pf, = _norm_proj(hp, norm_mix[l], w_in, ones, (), tm_p, False)
            chunk = math.gcd(t, HG_CHUNK)
            o_p, st_p = _hgrn(pf.reshape(b, t, -1), lower_bounds[l], hgrn_norm_gain[r],
                              jnp.zeros((b, HG_HEADS, HG_DK, HG_DK), F32), chunk, _row_tile(t, 512), None)
            hp = _out_proj(o_p.reshape(b * t, -1), 0, o_p.reshape(b * t, -1), 1, hp, w_out, tm_p)
            hg_p.append(jnp.swapaxes(st_p, -1, -2))
            sf, = _norm_proj(hs, norm_mix[l], w_in, ones, (), tm_s, False)
            o_s, st_s = _hgrn(sf.reshape(s, tsp, -1), lower_bounds[l], hgrn_norm_gain[r],
                              jnp.swapaxes(state_hgrn[r].astype(F32), -1, -2), tsp, tsp, ts)
            hs = _out_proj(o_s.reshape(s * tsp, -1), 0, o_s.reshape(s * tsp, -1), 1, hs, w_out, tm_s)
            hg_s.append(jnp.swapaxes(st_s, -1, -2))
        w_fi = w_ffn_in[l].astype(BF16)
        w_fo = w_ffn_out[l].astype(BF16)
        tf = w_fo.shape[0] // 2
        hp = _ffn(hp, norm_ffn[l], w_fi, w_fo, tm_p, tf)
        hs = _ffn(hs, norm_ffn[l], w_fi, w_fo, tm_s, tf)

    y_s = hs.reshape(s, tsp, d)[:, :ts]
    return (hp.reshape(b, t, d), y_s, jnp.stack(dkv_p), jnp.stack(dkv_s), jnp.stack(nkv_p), jnp.stack(nkv_s),
            jnp.stack(win_p), jnp.stack(win_s), jnp.stack(hg_p), jnp.stack(hg_s))
```

```python
import functools
import math

import jax
import jax.numpy as jnp
from jax import lax
from jax.experimental import pallas as pl
from jax.experimental.pallas import tpu as pltpu

F32 = jnp.float32
BF16 = jnp.bfloat16

D_MODEL = 1024
HEAD_DIM = 64
DA_HEADS = 4
DA_VDIM = 2 * HEAD_DIM
NSA_HEADS = 8
NSA_GROUPS = 2
NSA_REP = NSA_HEADS // NSA_GROUPS
NSA_BLOCK = 64
NSA_TOPK = 16
NSA_WINDOW = 512
NSA_CMP_HID = 2 * HEAD_DIM
HG_HEADS = 8
HG_DK = D_MODEL // HG_HEADS
HG_CHUNK = 64
EPS = 1e-6
NEG = -1e30
M_FLOOR = -1e29
LOG2E = 1.4426950408889634

LANES = 128
SUBLANES = 8
VMEM_LIMIT = 56 * 1024 * 1024

C_DQ, C_DK, C_DV, C_NQ, C_NKV, C_NG = 0, 512, 1024, 1536, 2048, 2816
EVEN_IN = 2840
EVEN_IN_PAD = 2944
EVEN_NORM_BLOCKS = tuple(range(0, 8)) + tuple(range(12, 16)) + (18, 20)
ODD_IN = 4096
SAMPLE_ROWS = 16


def _cparams(sem):
    return pltpu.CompilerParams(dimension_semantics=sem, vmem_limit_bytes=VMEM_LIMIT)


def _nt_dot(a, b):
    return lax.dot_general(a, b, (((1,), (1,)), ((), ())), preferred_element_type=F32)


def _rms(x, gain):
    return x * lax.rsqrt(jnp.mean(x * x, axis=-1, keepdims=True) + EPS) * gain


def _seg_rms(y, gain):
    lo = lax.broadcasted_iota(jnp.int32, y.shape, 1) < HEAD_DIM
    y2 = y * y
    s_lo = jnp.sum(jnp.where(lo, y2, 0.0), axis=-1, keepdims=True)
    s_hi = jnp.sum(jnp.where(lo, 0.0, y2), axis=-1, keepdims=True)
    ms = jnp.where(lo, s_lo, s_hi) * (1.0 / HEAD_DIM)
    return y * lax.rsqrt(ms + EPS) * gain


def _norm_proj_kernel(x_ref, g_ref, w_ref, cg_ref, o_ref, *maybe_ob_ref, norm_blocks, col_chunk):
    xn = _rms(x_ref[...], g_ref[...]).astype(BF16)
    n_cols = w_ref.shape[1]
    for c0 in range(0, n_cols, col_chunk):
        c1 = min(c0 + col_chunk, n_cols)
        y = jnp.dot(xn, w_ref[:, c0:c1], preferred_element_type=F32)
        for b in range(c0 // LANES, c1 // LANES):
            yb = y[:, b * LANES - c0:(b + 1) * LANES - c0]
            if b in norm_blocks:
                yb = _seg_rms(yb, cg_ref[:, b * LANES:(b + 1) * LANES])
            o_ref[:, b * LANES:(b + 1) * LANES] = yb
            for ob_ref in maybe_ob_ref:
                ob_ref[:, b * LANES:(b + 1) * LANES] = yb.astype(BF16)


def _norm_proj(x2d, gain, w_bf16, colgain, norm_blocks, tm, with_bf16):
    n, d = x2d.shape
    c = w_bf16.shape[1]
    kern = functools.partial(_norm_proj_kernel, norm_blocks=norm_blocks, col_chunk=512)
    n_out = 2 if with_bf16 else 1
    return pl.pallas_call(
        kern,
        grid=(n // tm,),
        in_specs=[
            pl.BlockSpec((tm, d), lambda i: (i, 0)),
            pl.BlockSpec((1, d), lambda i: (0, 0)),
            pl.BlockSpec((d, c), lambda i: (0, 0)),
            pl.BlockSpec((1, c), lambda i: (0, 0)),
        ],
        out_specs=[pl.BlockSpec((tm, c), lambda i: (i, 0))] * n_out,
        out_shape=[jax.ShapeDtypeStruct((n, c), F32), jax.ShapeDtypeStruct((n, c), BF16)][:n_out],
        compiler_params=_cparams(("parallel",)),
    )(x2d, gain.reshape(1, d), w_bf16, colgain)


def _ffn_kernel(x_ref, g_ref, wg_ref, wu_ref, wo_ref, o_ref, xn_scr, acc_scr):
    f = pl.program_id(1)

    @pl.when(f == 0)
    def _():
        xn_scr[...] = _rms(x_ref[...], g_ref[...]).astype(BF16)
        acc_scr[...] = jnp.zeros_like(acc_scr)

    xn = xn_scr[...]
    gate = jnp.dot(xn, wg_ref[...], preferred_element_type=F32)
    up = jnp.dot(xn, wu_ref[...], preferred_element_type=F32)
    act = (gate * jax.nn.sigmoid(gate) * up).astype(BF16)
    acc_scr[...] += jnp.dot(act, wo_ref[...], preferred_element_type=F32)

    @pl.when(f == pl.num_programs(1) - 1)
    def _():
        o_ref[...] = x_ref[...] + acc_scr[...]


def _ffn(x2d, gain, w_in_bf16, w_out_bf16, tm, tf):
    n, d = x2d.shape
    dff = w_out_bf16.shape[0]
    nf = dff // tf
    return pl.pallas_call(
        _ffn_kernel,
        grid=(n // tm, nf),
        in_specs=[
            pl.BlockSpec((tm, d), lambda i, f: (i, 0)),
            pl.BlockSpec((1, d), lambda i, f: (0, 0)),
            pl.BlockSpec((d, tf), lambda i, f: (0, f)),
            pl.BlockSpec((d, tf), lambda i, f: (0, nf + f)),
            pl.BlockSpec((tf, d), lambda i, f: (f, 0)),
        ],
        out_specs=pl.BlockSpec((tm, d), lambda i, f: (i, 0)),
        out_shape=jax.ShapeDtypeStruct((n, d), F32),
        scratch_shapes=[pltpu.VMEM((tm, d), BF16), pltpu.VMEM((tm, d), F32)],
        compiler_params=_cparams(("parallel", "arbitrary")),
    )(x2d, gain.reshape(1, d), w_in_bf16, w_in_bf16, w_out_bf16)


def _out_proj_kernel(a_ref, b_ref, r_ref, w_ref, o_ref):
    half = a_ref.shape[1]
    y = jnp.dot(a_ref[...].astype(BF16), w_ref[0:half, :], preferred_element_type=F32)
    y += jnp.dot(b_ref[...].astype(BF16), w_ref[half:2 * half, :], preferred_element_type=F32)
    o_ref[...] = r_ref[...] + y


def _out_proj(a, a_blk, b, b_blk, resid, w_bf16, tm):
    n, d = resid.shape
    half = w_bf16.shape[0] // 2
    return pl.pallas_call(
        _out_proj_kernel,
        grid=(n // tm,),
        in_specs=[
            pl.BlockSpec((tm, half), lambda i: (i, a_blk)),
            pl.BlockSpec((tm, half), lambda i: (i, b_blk)),
            pl.BlockSpec((tm, d), lambda i: (i, 0)),
            pl.BlockSpec((2 * half, d), lambda i: (0, 0)),
        ],
        out_specs=pl.BlockSpec((tm, d), lambda i: (i, 0)),
        out_shape=jax.ShapeDtypeStruct((n, d), F32),
        compiler_params=_cparams(("parallel",)),
    )(a, b, resid, w_bf16)


def _flash_init(m_scr, l_scr, acc_scr):
    m_scr[...] = jnp.full(m_scr.shape, NEG, F32)
    l_scr[...] = jnp.zeros(l_scr.shape, F32)
    acc_scr[...] = jnp.zeros(acc_scr.shape, F32)


def _flash_update(s, mask, v_tile, m_scr, l_scr, acc_scr):
    s = jnp.where(mask, s, NEG)
    m_old = m_scr[...]
    m_new = jnp.maximum(m_old, jnp.max(s, axis=-1, keepdims=True))
    alpha = jnp.exp(m_old - m_new)
    p = jnp.where(mask, jnp.exp(s - m_new), 0.0)
    l_scr[...] = alpha * l_scr[...] + jnp.sum(p, axis=-1, keepdims=True)
    acc_scr[...] = alpha * acc_scr[...] + jnp.dot(p.astype(BF16), v_tile, preferred_element_type=F32)
    m_scr[...] = m_new


def _diff_lambda(lp_ref, lam_init):
    lp = lp_ref[...].astype(F32)
    a = jnp.sum(lp[0:1, :] * lp[1:2, :], axis=-1, keepdims=True)
    b = jnp.sum(lp[2:3, :] * lp[3:4, :], axis=-1, keepdims=True)
    return jnp.exp(a) - jnp.exp(b) + lam_init


def _flash_init_t(m_scr, l_scr, acc_scr):
    m_scr[...] = jnp.full(m_scr.shape, M_FLOOR, F32)
    l_scr[...] = jnp.zeros(l_scr.shape, F32)
    acc_scr[...] = jnp.zeros(acc_scr.shape, F32)


def _flash_update_t(st, vt_tile, m_scr, l_scr, acc_scr):
    m_old = m_scr[...]
    m_new = jnp.maximum(m_old, jnp.max(st, axis=0, keepdims=True))
    alpha = jnp.exp2(m_old - m_new)
    p = jnp.exp2(st - m_new)
    l_scr[...] = alpha * l_scr[...] + jnp.sum(p, axis=0, keepdims=True)
    acc_scr[...] = alpha * acc_scr[...] + jnp.dot(vt_tile, p.astype(BF16), preferred_element_type=F32)
    m_scr[...] = m_new


def _diff_qstack(q, scale):
    lo = lax.broadcasted_iota(jnp.int32, q.shape, 1) < HEAD_DIM
    qs = q.astype(F32) * scale
    return jnp.concatenate([jnp.where(lo, qs, 0.0), jnp.where(lo, 0.0, qs)], axis=0)


def _diff_finish(m_scr, l_scr, acc_scr, lam, gain, lam_init, tq):
    o = acc_scr[...] / jnp.maximum(l_scr[...], 1e-30)
    d = o[0:tq] - lam * o[tq:2 * tq]
    return _rms(d, gain) * (1.0 - lam_init)


def _diff_prompt_kernel(q_ref, k_ref, vt_ref, lp_ref, sg_ref, o_ref, m_scr, l_scr, acc_scr,
                        *, tq, tk, lam_init):
    assert tq & (tq - 1) == 0
    q0 = pl.program_id(2) * tq
    qst = _diff_qstack(q_ref[...], HEAD_DIM ** -0.5 * LOG2E).T.astype(BF16)
    lane = lax.broadcasted_iota(jnp.int32, (1, 2 * tq), 1)
    qpos = q0 + (lane & (tq - 1))
    _flash_init_t(m_scr, l_scr, acc_scr)

    def tile(ki, causal):
        k0 = pl.multiple_of(ki * tk, tk)
        st = jnp.dot(k_ref[pl.ds(k0, tk), :], qst, preferred_element_type=F32)
        if causal:
            kpos = k0 + lax.broadcasted_iota(jnp.int32, (tk, 1), 0)
            st = jnp.where(kpos <= qpos, st, NEG)
        _flash_update_t(st, vt_ref[:, pl.ds(k0, tk)], m_scr, l_scr, acc_scr)

    def full_tile(ki, carry):
        tile(ki, False)
        return carry

    def diag_tile(ki, carry):
        tile(ki, True)
        return carry

    n_full = q0 // tk
    lax.fori_loop(0, n_full, full_tile, 0)
    lax.fori_loop(n_full, (q0 + tq - 1) // tk + 1, diag_tile, 0)
    lam = _diff_lambda(lp_ref, lam_init)
    ot = acc_scr[...] / jnp.maximum(l_scr[...], 1e-30)
    dt = ot[:, 0:tq] - lam * ot[:, tq:2 * tq]
    dn = dt * lax.rsqrt(jnp.mean(dt * dt, axis=0, keepdims=True) + EPS)
    o_ref[...] = dn.T * sg_ref[...] * (1.0 - lam_init)


def _diff_prompt(projb, vt, lam_par, subln, lam_init, tq, tk):
    b, t, _ = projb.shape
    kern = functools.partial(_diff_prompt_kernel, tq=tq, tk=tk, lam_init=lam_init)
    kblk = C_DK // LANES
    return pl.pallas_call(
        kern,
        grid=(b, DA_HEADS, t // tq),
        in_specs=[
            pl.BlockSpec((None, tq, LANES), lambda bi, h, qi: (bi, qi, h)),
            pl.BlockSpec((None, t, LANES), lambda bi, h, qi: (bi, 0, kblk + h)),
            pl.BlockSpec((None, DA_VDIM, t), lambda bi, h, qi: (bi, h, 0)),
            pl.BlockSpec((4, HEAD_DIM), lambda bi, h, qi: (0, 0)),
            pl.BlockSpec((1, DA_VDIM), lambda bi, h, qi: (0, 0)),
        ],
        out_specs=pl.BlockSpec((None, tq, LANES), lambda bi, h, qi: (bi, qi, h)),
        out_shape=jax.ShapeDtypeStruct((b, t, DA_HEADS * DA_VDIM), F32),
        scratch_shapes=[pltpu.VMEM((1, 2 * tq), F32), pltpu.VMEM((1, 2 * tq), F32),
                        pltpu.VMEM((DA_VDIM, 2 * tq), F32)],
        compiler_params=_cparams(("parallel", "parallel", "arbitrary")),
    )(projb, projb, vt, lam_par, subln.reshape(1, DA_VDIM))


def _page_copies(cache_ref, e, pt_ref, seq, page0, n_pages, lane0, n_lanes, buf, sem, page_size):
    copies = []
    for j in range(n_pages):
        page = pt_ref[seq, page0 + j]
        copies.append(pltpu.make_async_copy(
            cache_ref.at[e, page, :, pl.ds(lane0, n_lanes)],
            buf.at[pl.ds(j * page_size, page_size), :],
            sem.at[j]))
    return copies


def _diff_sample_kernel(pt_ref, q_ref, tail_ref, lp_ref, sg_ref, cache_ref, o_ref,
                        buf0, buf1, sem0, sem1, m_scr, l_scr, acc_scr,
                        *, e, n_chunks, pages_per_chunk, page_size, past_len, tk, lam_init):
    s_idx = pl.program_id(0)
    c_idx = pl.program_id(1)
    bufs, sems = (buf0, buf1), (sem0, sem1)
    tq = SAMPLE_ROWS
    width = DA_HEADS * DA_VDIM
    assert tq & (tq - 1) == 0
    n_rows = DA_HEADS * 2 * tq
    row = lax.broadcasted_iota(jnp.int32, (n_rows, 1), 0)
    qpos = past_len + (row & (tq - 1))

    @pl.when(c_idx == 0)
    def _():
        _flash_init(m_scr, l_scr, acc_scr)

    qs = [_diff_qstack(q_ref[:, h * LANES:(h + 1) * LANES], HEAD_DIM ** -0.5).astype(BF16)
          for h in range(DA_HEADS)]

    def update(k_tile, v_tile, kpos):
        s = jnp.concatenate([_nt_dot(qs[h], k_tile(h).astype(BF16)) for h in range(DA_HEADS)], axis=0)
        mask = kpos <= qpos
        s = jnp.where(mask, s, NEG)
        m_old = m_scr[...]
        m_new = jnp.maximum(m_old, jnp.max(s, axis=-1, keepdims=True))
        alpha = jnp.exp(m_old - m_new)
        p = jnp.where(mask, jnp.exp(s - m_new), 0.0)
        l_scr[...] = alpha * l_scr[...] + jnp.sum(p, axis=-1, keepdims=True)
        pb = p.astype(BF16)
        pv = jnp.concatenate([jnp.dot(pb[h * 2 * tq:(h + 1) * 2 * tq], v_tile(h).astype(BF16),
                                      preferred_element_type=F32) for h in range(DA_HEADS)], axis=0)
        acc_scr[...] = alpha * acc_scr[...] + pv
        m_scr[...] = m_new

    @pl.when(c_idx < n_chunks)
    def _():
        per_pos = 2 * DA_HEADS
        n_seq = pl.num_programs(0)
        chunk_rows = pages_per_chunk * page_size

        def chunk_copies(seq, chunk, slot):
            return _page_copies(cache_ref, e, pt_ref, seq, chunk * pages_per_chunk, pages_per_chunk,
                                0, LANES, bufs[slot], sems[slot], page_size * per_pos)

        def run(slot):
            @pl.when((s_idx == 0) & (c_idx == 0))
            def _():
                for cp in chunk_copies(s_idx, c_idx, slot):
                    cp.start()

            @pl.when(c_idx + 1 < n_chunks)
            def _():
                for cp in chunk_copies(s_idx, c_idx + 1, 1 - slot):
                    cp.start()

            @pl.when((c_idx + 1 == n_chunks) & (s_idx + 1 < n_seq))
            def _():
                for cp in chunk_copies(s_idx + 1, 0, 1 - slot):
                    cp.start()

            for cp in chunk_copies(s_idx, c_idx, slot):
                cp.wait()
            buf = bufs[slot]
            for r0 in range(0, chunk_rows, tk):
                kpos = (c_idx * chunk_rows + r0) + lax.broadcasted_iota(jnp.int32, (1, tk), 1)
                update(lambda h: buf[pl.ds(r0 * per_pos + h, tk, stride=per_pos), :],
                       lambda h: buf[pl.ds(r0 * per_pos + DA_HEADS + h, tk, stride=per_pos), :], kpos)

        parity = (s_idx * n_chunks + c_idx) & 1
        for slot in range(2):
            pl.when(parity == slot)(functools.partial(run, slot))

    @pl.when(c_idx == n_chunks)
    def _():
        n_tail = tail_ref.shape[0]
        kpos = past_len + lax.broadcasted_iota(jnp.int32, (1, n_tail), 1)
        lam = _diff_lambda(lp_ref, lam_init)
        update(lambda h: tail_ref[:, h * LANES:(h + 1) * LANES],
               lambda h: tail_ref[:, width + h * LANES:width + (h + 1) * LANES], kpos)
        for h in range(DA_HEADS):
            hs = slice(h * 2 * tq, (h + 1) * 2 * tq)
            o_ref[:, h * LANES:(h + 1) * LANES] = _diff_finish(
                m_scr.at[hs], l_scr.at[hs], acc_scr.at[hs], lam, sg_ref[...], lam_init, tq)


def _diff_sample(projb_s, tail, cache4, e, page_table, lam_par, subln, lam_init, past_len):
    s, tq, _ = projb_s.shape
    n_pages = page_table.shape[1]
    page_size = cache4.shape[2] // (2 * DA_HEADS)
    pages_per_chunk = math.gcd(n_pages, 16)
    n_chunks = n_pages // pages_per_chunk
    width = DA_HEADS * DA_VDIM
    tk = math.gcd(pages_per_chunk * page_size, 512)
    kern = functools.partial(_diff_sample_kernel, e=e, n_chunks=n_chunks, pages_per_chunk=pages_per_chunk,
                             page_size=page_size, past_len=past_len, tk=tk, lam_init=lam_init)
    grid_spec = pltpu.PrefetchScalarGridSpec(
        num_scalar_prefetch=1,
        grid=(s, n_chunks + 1),
        in_specs=[
            pl.BlockSpec((None, tq, width), lambda si, ci, pt: (si, 0, 0)),
            pl.BlockSpec((None, tail.shape[1], 2 * width), lambda si, ci, pt: (si, 0, 0)),
            pl.BlockSpec((4, HEAD_DIM), lambda si, ci, pt: (0, 0)),
            pl.BlockSpec((1, DA_VDIM), lambda si, ci, pt: (0, 0)),
            pl.BlockSpec(memory_space=pl.ANY),
        ],
        out_specs=pl.BlockSpec((None, tq, width), lambda si, ci, pt: (si, 0, 0)),
        scratch_shapes=[
            pltpu.VMEM((pages_per_chunk * page_size * 2 * DA_HEADS, LANES), F32),
            pltpu.VMEM((pages_per_chunk * page_size * 2 * DA_HEADS, LANES), F32),
            pltpu.SemaphoreType.DMA((pages_per_chunk,)),
            pltpu.SemaphoreType.DMA((pages_per_chunk,)),
            pltpu.VMEM((DA_HEADS * 2 * tq, 1), F32),
            pltpu.VMEM((DA_HEADS * 2 * tq, 1), F32),
            pltpu.VMEM((DA_HEADS * 2 * tq, LANES), F32),
        ],
    )
    return pl.pallas_call(
        kern,
        grid_spec=grid_spec,
        out_shape=jax.ShapeDtypeStruct((s, tq, width), F32),
        compiler_params=_cparams(("arbitrary", "arbitrary")),
    )(page_table, projb_s, tail, lam_par, subln.reshape(1, DA_VDIM), cache4)


def _compress_compute(xk_ref, xv_ref, pe_ref, w1_ref, w2_ref, kg_ref, kcb_ref, vcb_ref, nb):
    def body(l, accs):
        pe = pe_ref[pl.ds(l, 1), :]
        ak = (xk_ref[pl.ds(l, nb, stride=NSA_BLOCK), :] + pe[:, 0:LANES]).astype(BF16)
        av = (xv_ref[pl.ds(l, nb, stride=NSA_BLOCK), :] + pe[:, LANES:2 * LANES]).astype(BF16)
        hk = jnp.dot(ak, w1_ref[0, l], preferred_element_type=F32)
        hv = jnp.dot(av, w1_ref[1, l], preferred_element_type=F32)
        return accs[0] + hk, accs[1] + hv

    zero = jnp.zeros((nb, 2 * NSA_CMP_HID), F32)
    hk, hv = lax.fori_loop(0, NSA_BLOCK, body, (zero, zero))
    ck = jnp.dot(jax.nn.gelu(hk).astype(BF16), w2_ref[0], preferred_element_type=F32)
    cv = jnp.dot(jax.nn.gelu(hv).astype(BF16), w2_ref[1], preferred_element_type=F32)
    ck = _seg_rms(ck, kg_ref[...])
    for ref in (kcb_ref, vcb_ref):
        ref[...] = jnp.zeros(ref.shape, ref.dtype)
    kcb_ref[0:nb, :] = ck.astype(BF16)
    vcb_ref[0:nb, :] = cv.astype(BF16)


def _compress_prompt_kernel(xk_ref, xv_ref, pe_ref, w1_ref, w2_ref, kg_ref, kcb_ref, vcb_ref, *, nb):
    _compress_compute(xk_ref, xv_ref, pe_ref, w1_ref, w2_ref, kg_ref, kcb_ref, vcb_ref, nb)


def _compress_specs(nbp):
    hid2 = 2 * NSA_CMP_HID
    w_specs = [
        pl.BlockSpec((NSA_BLOCK, 2 * LANES), lambda *a: (0, 0)),
        pl.BlockSpec((2, NSA_BLOCK, LANES, hid2), lambda *a: (0, 0, 0, 0)),
        pl.BlockSpec((2, hid2, LANES), lambda *a: (0, 0, 0)),
        pl.BlockSpec((1, LANES), lambda *a: (0, 0)),
    ]
    out_specs = [pl.BlockSpec((None, nbp, LANES), lambda bi, *a: (bi, 0, 0))] * 2
    return w_specs, out_specs


def _compress_out_shape(nbat, nbp):
    return [jax.ShapeDtypeStruct((nbat, nbp, LANES), BF16)] * 2


def _compress_prompt(projf, cw):
    b, t, _ = projf.shape
    nb = t // NSA_BLOCK
    nbp = -(-nb // LANES) * LANES
    w_specs, out_specs = _compress_specs(nbp)
    return pl.pallas_call(
        functools.partial(_compress_prompt_kernel, nb=nb),
        grid=(b,),
        in_specs=[pl.BlockSpec((None, t, LANES), lambda bi: (bi, 0, C_NKV // LANES)),
                  pl.BlockSpec((None, t, LANES), lambda bi: (bi, 0, C_NKV // LANES + 1))] + w_specs,
        out_specs=out_specs,
        out_shape=_compress_out_shape(b, nbp),
        compiler_params=_cparams(("parallel",)),
    )(projf, projf, *cw)


def _feature_page_copies(cache_ref, e, pt_ref, seq, n_pages, row0, n_rows, buf, sem):
    copies = []
    for j in range(n_pages):
        copies.append(pltpu.make_async_copy(
            cache_ref.at[e, pt_ref[seq, j], pl.ds(row0, n_rows), :],
            buf.at[pl.ds(j * n_rows, n_rows), :],
            sem.at[j]))
    return copies


def _compress_sample_kernel(pt_ref, tail_ref, pe_ref, w1_ref, w2_ref, kg_ref, cache_ref,
                            kcb_ref, vcb_ref, buf, sem, *, e, n_pages, half_stride):
    s_idx = pl.program_id(0)
    feat = 2 * LANES
    copies = _feature_page_copies(cache_ref, e, pt_ref, s_idx, n_pages, 0, feat, buf, sem)
    for cp in copies:
        cp.start()
    buf[n_pages * feat:, :] = tail_ref[...]
    for cp in copies:
        cp.wait()
    n_pp = n_pages + 1

    def body(d, accs):
        out = []
        for c in range(2):
            for g in range(NSA_GROUPS):
                f = c * LANES + g * HEAD_DIM + d
                a = (buf[pl.ds(f, n_pp, stride=feat), :] + pe_ref[pl.ds(f, 1), :]).astype(BF16)
                out.append(accs[c * NSA_GROUPS + g] + jnp.dot(a, w1_ref[c, d], preferred_element_type=F32))
        return tuple(out)

    zero = jnp.zeros((n_pp, 2 * NSA_CMP_HID), F32)
    accs = lax.fori_loop(0, HEAD_DIM, body, (zero,) * (2 * NSA_GROUPS))
    pad = jnp.zeros((-n_pp % (2 * SUBLANES), LANES), F32)
    for c, out_ref in ((0, kcb_ref), (1, vcb_ref)):
        out_ref[...] = jnp.zeros(out_ref.shape, out_ref.dtype)
        for half in range(2):
            hs = slice(half * NSA_CMP_HID, (half + 1) * NSA_CMP_HID)
            hid = jnp.concatenate([accs[c * NSA_GROUPS + g][:, hs] for g in range(NSA_GROUPS)], axis=1)
            y = jnp.dot(jax.nn.gelu(hid).astype(BF16), w2_ref[c], preferred_element_type=F32)
            if c == 0:
                y = _seg_rms(y, kg_ref[...])
            y = jnp.concatenate([y, pad], axis=0).astype(BF16)
            out_ref[half * half_stride:half * half_stride + y.shape[0], :] = y


def _compress_sample(tail_t, cache_t, e, page_table, cw_t):
    s = tail_t.shape[0]
    n_pages = page_table.shape[1]
    page_size = cache_t.shape[3]
    assert page_size == 2 * NSA_BLOCK
    feat = 2 * LANES
    half_stride = -(-(n_pages + 1) // LANES) * LANES
    nbp = 2 * half_stride
    hid2 = 2 * NSA_CMP_HID
    grid_spec = pltpu.PrefetchScalarGridSpec(
        num_scalar_prefetch=1,
        grid=(s,),
        in_specs=[
            pl.BlockSpec((None, feat, page_size), lambda si, pt: (si, 0, 0)),
            pl.BlockSpec((feat, page_size), lambda si, pt: (0, 0)),
            pl.BlockSpec((2, HEAD_DIM, LANES, hid2), lambda si, pt: (0, 0, 0, 0)),
            pl.BlockSpec((2, hid2, LANES), lambda si, pt: (0, 0, 0)),
            pl.BlockSpec((1, LANES), lambda si, pt: (0, 0)),
            pl.BlockSpec(memory_space=pl.ANY),
        ],
        out_specs=[pl.BlockSpec((None, nbp, LANES), lambda si, pt: (si, 0, 0))] * 2,
        scratch_shapes=[pltpu.VMEM(((n_pages + 1) * feat, page_size), F32),
                        pltpu.SemaphoreType.DMA((n_pages,))],
    )
    return pl.pallas_call(
        functools.partial(_compress_sample_kernel, e=e, n_pages=n_pages, half_stride=half_stride),
        grid_spec=grid_spec,
        out_shape=_compress_out_shape(s, nbp),
        compiler_params=_cparams(("arbitrary",)),
    )(page_table, tail_t, *cw_t, cache_t)


def _nsa_branches(q, kc_ref, vc_ref, k_sel_tile, vt_sel_tile, k_win_tile, vt_win_tile,
                  sel_scr, m_s, l_s, a_s, m_w, l_w, a_w,
                  *, tq, q0, n_blk, n_sel_rows, tk, wbase, n_win_rows, tkw,
                  row_block=lambda row: row, block_row=lambda blk: blk):
    nh = NSA_HEADS
    n_rows = nh * tq
    blk_shift = NSA_BLOCK.bit_length() - 1
    tq_shift = tq.bit_length() - 1
    assert 1 << tq_shift == tq and 1 << blk_shift == NSA_BLOCK
    lane128 = lax.broadcasted_iota(jnp.int32, (tq, LANES), 1)
    qf = q.astype(F32) * (HEAD_DIM ** -0.5 * LOG2E)
    parts = []
    for hh in range(nh):
        g = hh // NSA_REP
        blk = qf[:, (hh // 2) * LANES:(hh // 2 + 1) * LANES]
        if hh % 2 != g:
            blk = pltpu.roll(blk, HEAD_DIM, 1)
        in_seg = (lane128 >= g * HEAD_DIM) & (lane128 < (g + 1) * HEAD_DIM)
        parts.append(jnp.where(in_seg, blk, 0.0))
    qst = jnp.concatenate(parts, axis=0).T.astype(BF16)

    lane = lax.broadcasted_iota(jnp.int32, (1, n_rows), 1)
    qpos = q0 + (lane & (tq - 1))

    nbp = kc_ref.shape[0]
    nb8 = sel_scr.shape[0]
    assert n_blk <= nb8 <= nbp
    sc = jnp.dot(kc_ref[...], qst, preferred_element_type=F32)
    blk_p = row_block(lax.broadcasted_iota(jnp.int32, (nbp, 1), 0))
    cmp_ok = blk_p < ((qpos + 1) >> blk_shift)
    sc = jnp.where(cmp_ok, sc, NEG)
    mc = jnp.max(sc, axis=0, keepdims=True)
    ec = jnp.where(cmp_ok, jnp.exp2(sc - mc), 0.0)
    pc_all = ec / jnp.maximum(jnp.sum(ec, axis=0, keepdims=True), 1e-30)
    vct = vc_ref[...].astype(F32).T.astype(BF16)
    o_cmp = jnp.dot(vct, pc_all.astype(BF16), preferred_element_type=F32)
    pc = pc_all[0:nb8]
    blk_i = blk_p[0:nb8]
    blk_f = blk_i.astype(F32)

    n_pick = min(NSA_TOPK - 1, n_blk)

    def pick(imp, cur):
        cand = blk_i < cur
        x = jnp.where(cand, imp, -1.0)
        x = jnp.where(blk_i < n_blk, x, -2.0)
        picked = jnp.zeros(x.shape, F32)
        for _ in range(n_pick):
            mx = jnp.max(x, axis=0, keepdims=True)
            first = jnp.min(jnp.where(x == mx, blk_f, 3e38), axis=0, keepdims=True)
            hit = blk_f == first
            picked = jnp.where(hit, 1.0, picked)
            x = jnp.where(hit, -3.0, x)
        chosen = (blk_i == cur) | (cand & (picked > 0.5))
        return jnp.where(chosen, 0.0, NEG)

    if tq % LANES == 0:
        cur_t = (q0 + lax.broadcasted_iota(jnp.int32, (1, tq), 1)) >> blk_shift
        for g in range(NSA_GROUPS):
            base = g * NSA_REP * tq
            imp = pc[:, base:base + tq]
            for r in range(1, NSA_REP):
                imp = imp + pc[:, base + r * tq:base + (r + 1) * tq]
            sel_g = pick(imp, cur_t)
            for r in range(NSA_REP):
                sel_scr[:, base + r * tq:base + (r + 1) * tq] = sel_g
    else:
        assert n_rows == LANES
        rep = (lane >> tq_shift) & (NSA_REP - 1)
        imp = pc
        for d in range(1, NSA_REP):
            up = pltpu.roll(pc, LANES - d * tq, 1)
            dn = pltpu.roll(pc, d * tq, 1)
            imp = imp + jnp.where(rep + d < NSA_REP, up, 0.0) + jnp.where(rep >= d, dn, 0.0)
        sel_scr[...] = pick(imp, qpos >> blk_shift)

    _flash_init_t(m_s, l_s, a_s)
    blocks_per_tile = tk // NSA_BLOCK

    def sel_tile(ki, causal):
        k0 = pl.multiple_of(ki * tk, tk)
        st = jnp.dot(k_sel_tile(k0), qst, preferred_element_type=F32)
        b0 = ki * blocks_per_tile
        kpos = k0 + lax.broadcasted_iota(jnp.int32, (tk, 1), 0)
        pieces = []
        for bb in range(blocks_per_tile):
            rs = slice(bb * NSA_BLOCK, (bb + 1) * NSA_BLOCK)
            piece = st[rs] + sel_scr[pl.ds(block_row(b0 + bb), 1), :]
            pieces.append(jnp.where(kpos[rs] <= qpos, piece, NEG) if causal else piece)
        _flash_update_t(jnp.concatenate(pieces, axis=0), vt_sel_tile(k0), m_s, l_s, a_s)

    def sel_full(ki, carry):
        sel_tile(ki, False)
        return carry

    def sel_diag(ki, carry):
        sel_tile(ki, True)
        return carry

    n_sel = jnp.minimum((q0 + tq - 1) // tk, n_sel_rows // tk - 1) + 1
    n_full = jnp.minimum(q0 // tk, n_sel)
    lax.fori_loop(0, n_full, sel_full, 0)
    lax.fori_loop(n_full, n_sel, sel_diag, 0)

    _flash_init_t(m_w, l_w, a_w)

    def win_body(ki, carry):
        k0 = pl.multiple_of(ki * tkw, tkw)
        st = jnp.dot(k_win_tile(k0), qst, preferred_element_type=F32)
        kpos = wbase + k0 + lax.broadcasted_iota(jnp.int32, (tkw, 1), 0)
        kpos = jnp.where(kpos >= 0, kpos, jnp.iinfo(jnp.int32).max)
        dist = qpos - kpos
        ok = (dist >= 0) & (dist <= NSA_WINDOW)
        _flash_update_t(jnp.where(ok, st, NEG), vt_win_tile(k0), m_w, l_w, a_w)
        return carry

    lo_tile = jnp.maximum(q0 - NSA_WINDOW - wbase, 0) // tkw
    hi_tile = jnp.minimum((q0 + tq - 1 - wbase) // tkw, n_win_rows // tkw - 1)
    lax.fori_loop(lo_tile, hi_tile + 1, win_body, 0)

    o_sel = a_s[...] / jnp.maximum(l_s[...], 1e-30)
    o_win = a_w[...] / jnp.maximum(l_w[...], 1e-30)
    return o_cmp, o_sel, o_win


def _nsa_gate_lanes(o_cmp, o_sel, o_win, gate_logits, tq):
    gt = jax.nn.sigmoid(gate_logits).T
    def gate_row(branch):
        return jnp.concatenate([gt[3 * hh + branch:3 * hh + branch + 1, :] for hh in range(NSA_HEADS)], axis=1)
    ot = gate_row(0) * o_cmp + gate_row(1) * o_sel + gate_row(2) * o_win
    outs = []
    for j in range(NSA_HEADS // 2):
        g = (2 * j) // NSA_REP
        rs = slice(g * HEAD_DIM, (g + 1) * HEAD_DIM)
        pair = jnp.concatenate([ot[rs, (2 * j) * tq:(2 * j + 1) * tq],
                                ot[rs, (2 * j + 1) * tq:(2 * j + 2) * tq]], axis=0)
        outs.append(pair.T)
    return jnp.concatenate(outs, axis=1)


def _nsa_gate_rows(o_cmp, o_sel, o_win, gate_logits, tq):
    oc, os_, ow = o_cmp.T, o_sel.T, o_win.T
    gates = jax.nn.sigmoid(gate_logits)
    lo_half = lax.broadcasted_iota(jnp.int32, (tq, LANES), 1) < HEAD_DIM
    heads = []
    for hh in range(NSA_HEADS):
        g = hh // NSA_REP
        rs = slice(hh * tq, (hh + 1) * tq)
        o = (gates[:, 3 * hh:3 * hh + 1] * oc[rs] + gates[:, 3 * hh + 1:3 * hh + 2] * os_[rs]
             + gates[:, 3 * hh + 2:3 * hh + 3] * ow[rs])
        if hh % 2 != g:
            o = pltpu.roll(o, HEAD_DIM, 1)
        heads.append(o)
    return jnp.concatenate([jnp.where(lo_half, heads[2 * j], heads[2 * j + 1])
                            for j in range(NSA_HEADS // 2)], axis=1)


def _nsa_scratch_t(tq, nb8):
    n_rows = NSA_HEADS * tq
    one = [pltpu.VMEM((1, n_rows), F32), pltpu.VMEM((1, n_rows), F32), pltpu.VMEM((LANES, n_rows), F32)]
    return [pltpu.VMEM((nb8, n_rows), F32)] + one + one


def _nsa_prompt_kernel_t(q_ref, g_ref, kc_ref, vc_ref, ks_ref, vst_ref, kw_ref, vwt_ref, o_ref,
                         sel_scr, m_s, l_s, a_s, m_w, l_w, a_w, *, tq, tk, tkw, n_blk, t):
    q0 = pl.program_id(1) * tq
    branches = _nsa_branches(
        q_ref[...], kc_ref, vc_ref,
        lambda k0: ks_ref[pl.ds(k0, tk), :], lambda k0: vst_ref[:, pl.ds(k0, tk)],
        lambda k0: kw_ref[pl.ds(k0, tkw), :], lambda k0: vwt_ref[:, pl.ds(k0, tkw)],
        sel_scr, m_s, l_s, a_s, m_w, l_w, a_w,
        tq=tq, q0=q0, n_blk=n_blk, n_sel_rows=t, tk=tk, wbase=0, n_win_rows=t, tkw=tkw)
    o_ref[...] = _nsa_gate_lanes(*branches, g_ref[...], tq)


def _nsa_prompt_t(projf, projb, vst, vwt, kcb, vcb, tq, tk):
    b, t, _ = projb.shape
    nbp = kcb.shape[1]
    n_blk = t // NSA_BLOCK
    nb8 = -(-n_blk // (2 * SUBLANES)) * (2 * SUBLANES)
    tkw = min(tq, LANES)
    kern = functools.partial(_nsa_prompt_kernel_t, tq=tq, tk=tk, tkw=tkw, n_blk=n_blk, t=t)
    ks_blk = (C_NKV + 2 * LANES) // LANES
    kw_blk = (C_NKV + 4 * LANES) // LANES
    return pl.pallas_call(
        kern,
        grid=(b, t // tq),
        in_specs=[
            pl.BlockSpec((None, tq, 512), lambda bi, qi: (bi, qi, C_NQ // 512)),
            pl.BlockSpec((None, tq, LANES), lambda bi, qi: (bi, qi, C_NG // LANES)),
            pl.BlockSpec((None, nbp, LANES), lambda bi, qi: (bi, 0, 0)),
            pl.BlockSpec((None, nbp, LANES), lambda bi, qi: (bi, 0, 0)),
            pl.BlockSpec((None, t, LANES), lambda bi, qi: (bi, 0, ks_blk)),
            pl.BlockSpec((None, LANES, t), lambda bi, qi: (bi, 0, 0)),
            pl.BlockSpec((None, t, LANES), lambda bi, qi: (bi, 0, kw_blk)),
            pl.BlockSpec((None, LANES, t), lambda bi, qi: (bi, 0, 0)),
        ],
        out_specs=pl.BlockSpec((None, tq, 512), lambda bi, qi: (bi, qi, 0)),
        out_shape=jax.ShapeDtypeStruct((b, t, NSA_HEADS * HEAD_DIM), F32),
        scratch_shapes=_nsa_scratch_t(tq, nb8),
        compiler_params=_cparams(("parallel", "arbitrary")),
    )(projb, projf, kcb, vcb, projb, vst, projb, vwt)


def _nsa_sample_kernel_t(pt_ref, q_ref, g_ref, kc_ref, vc_ref, tail_ref, kwv_ref, cache_ref, o_ref,
                         buf, sem, sel_scr, m_s, l_s, a_s, m_w, l_w, a_w,
                         *, e, n_pages, page_size, past_len, n_blk, tkw, wbase, half_stride, pages_per_tile):
    s_idx = pl.program_id(0)
    feat = 2 * LANES
    copies = _feature_page_copies(cache_ref, e, pt_ref, s_idx, n_pages, feat, feat, buf, sem)
    for cp in copies:
        cp.start()
    buf[n_pages * feat:, :] = tail_ref[...]
    for cp in copies:
        cp.wait()
    tq = q_ref.shape[0]
    tk = pages_per_tile * page_size

    def page_rows(k0, j, first):
        return pl.ds(pl.multiple_of((k0 // page_size + j) * feat + first, LANES), LANES)

    def k_tile(k0):
        return jnp.concatenate([buf[page_rows(k0, j, 0), :].T for j in range(pages_per_tile)],
                               axis=0).astype(BF16)

    def vt_tile(k0):
        return jnp.concatenate([buf[page_rows(k0, j, LANES), :] for j in range(pages_per_tile)],
                               axis=1).astype(BF16)

    def row_block(row):
        half = (row >= half_stride).astype(jnp.int32)
        page = row - half * half_stride
        return jnp.where(page <= n_pages, 2 * page + half, jnp.iinfo(jnp.int32).max)

    branches = _nsa_branches(
        q_ref[...], kc_ref, vc_ref,
        k_tile, vt_tile,
        lambda k0: kwv_ref[pl.ds(k0, tkw), 0:LANES].astype(BF16),
        lambda k0: kwv_ref[pl.ds(k0, tkw), LANES:2 * LANES].T.astype(BF16),
        sel_scr, m_s, l_s, a_s, m_w, l_w, a_w,
        tq=tq, q0=past_len, n_blk=n_blk, n_sel_rows=(n_pages + 1) * page_size, tk=tk,
        wbase=wbase, n_win_rows=kwv_ref.shape[0], tkw=tkw,
        row_block=row_block, block_row=lambda blk: (blk & 1) * half_stride + (blk >> 1))
    o_ref[...] = _nsa_gate_rows(*branches, g_ref[...], tq)


def _nsa_sample_t(projf_s, projb_s, kcb, vcb, sel_tail_t, kwv, cache_t, e, page_table, past_len, n_valid):
    s, tq, _ = projb_s.shape
    n_pages = page_table.shape[1]
    page_size = cache_t.shape[3]
    feat = 2 * LANES
    nbp = kcb.shape[1]
    half_stride = nbp // 2
    n_blk = -(-(past_len + n_valid) // NSA_BLOCK)
    nb8 = nbp
    tkw = math.gcd(kwv.shape[1], LANES)
    wbase = past_len - (kwv.shape[1] - page_size)
    assert page_size == LANES
    pages_per_tile = max(p for p in range(1, 9) if (n_pages + 1) % p == 0)
    kern = functools.partial(_nsa_sample_kernel_t, e=e, n_pages=n_pages, page_size=page_size,
                             past_len=past_len, n_blk=n_blk, tkw=tkw, wbase=wbase, half_stride=half_stride,
                             pages_per_tile=pages_per_tile)
    grid_spec = pltpu.PrefetchScalarGridSpec(
        num_scalar_prefetch=1,
        grid=(s,),
        in_specs=[
            pl.BlockSpec((None, tq, 512), lambda si, pt: (si, 0, C_NQ // 512)),
            pl.BlockSpec((None, tq, LANES), lambda si, pt: (si, 0, C_NG // LANES)),
            pl.BlockSpec((None, nbp, LANES), lambda si, pt: (si, 0, 0)),
            pl.BlockSpec((None, nbp, LANES), lambda si, pt: (si, 0, 0)),
            pl.BlockSpec((None, feat, page_size), lambda si, pt: (si, 0, 0)),
            pl.BlockSpec((None, kwv.shape[1], 2 * LANES), lambda si, pt: (si, 0, 0)),
            pl.BlockSpec(memory_space=pl.ANY),
        ],
        out_specs=pl.BlockSpec((None, tq, 512), lambda si, pt: (si, 0, 0)),
        scratch_shapes=[pltpu.VMEM(((n_pages + 1) * feat, page_size), F32),
                        pltpu.SemaphoreType.DMA((n_pages,))] + _nsa_scratch_t(tq, nb8),
    )
    return pl.pallas_call(
        kern,
        grid_spec=grid_spec,
        out_shape=jax.ShapeDtypeStruct((s, tq, NSA_HEADS * HEAD_DIM), F32),
        compiler_params=_cparams(("arbitrary",)),
    )(page_table, projb_s, projf_s, kcb, vcb, sel_tail_t, kwv, cache_t)


def _split3(x):
    hi = x.astype(BF16)
    r1 = x - hi.astype(F32)
    mid = r1.astype(BF16)
    lo = (r1 - mid.astype(F32)).astype(BF16)
    return hi, mid, lo


def _hgrn_kernel(q_ref, f_ref, i_ref, g_ref, lb_ref, ng_ref, s0_ref, esel_ref, o_ref, sfin_ref,
                 st_scr, cum_scr, k_scr, *, chunk, n_chunks, n_valid):
    ti = pl.program_id(2)

    @pl.when(ti == 0)
    def _():
        st_scr[...] = s0_ref[...]

    lb = lb_ref[...]
    tri = (lax.broadcasted_iota(jnp.int32, (chunk, chunk), 0)
           >= lax.broadcasted_iota(jnp.int32, (chunk, chunk), 1)).astype(BF16)
    row = lax.broadcasted_iota(jnp.int32, (chunk, 1), 0)
    row8 = lax.broadcasted_iota(jnp.int32, (SUBLANES, 1), 0)
    n_sub = chunk // SUBLANES
    zero8 = jnp.zeros((SUBLANES, HG_DK), F32)

    def chunk_body(c, carry):
        r0 = pl.multiple_of(c * chunk, chunk)
        qr = q_ref[pl.ds(r0, chunk), :]
        q = qr * jax.nn.sigmoid(qr) * (HG_DK ** -0.5)
        fg = lb + (1.0 - lb) * jax.nn.sigmoid(f_ref[pl.ds(r0, chunk), :])
        logf = jnp.log(fg)
        k = 1.0 - fg
        v = i_ref[pl.ds(r0, chunk), :]
        if n_valid is not None:
            live = (ti * (chunk * n_chunks) + r0 + row) < n_valid
            logf = jnp.where(live, logf, 0.0)
            k = jnp.where(live, k, 0.0)
        cum = sum(jnp.dot(tri, part, preferred_element_type=F32) for part in _split3(logf))
        cum2 = cum * LOG2E
        cum_scr[...] = cum2
        k_scr[...] = k
        st = st_scr[...]
        inter = _nt_dot((q * jnp.exp2(cum2)).astype(BF16), st.astype(BF16))

        q_sub = [q[i * SUBLANES:(i + 1) * SUBLANES] for i in range(n_sub)]
        c_sub = [cum2[i * SUBLANES:(i + 1) * SUBLANES] for i in range(n_sub)]
        cols = []
        for s in range(chunk):
            j = s // SUBLANES
            cs = jnp.broadcast_to(cum_scr[s:s + 1, :], (SUBLANES, HG_DK))
            ks = jnp.broadcast_to(k_scr[s:s + 1, :], (SUBLANES, HG_DK))
            parts = []
            for i in range(n_sub):
                if i < j:
                    parts.append(zero8)
                    continue
                dec = jnp.exp2(c_sub[i] - cs)
                if i == j:
                    dec = jnp.where(row8 >= s - j * SUBLANES, dec, 0.0)
                parts.append(q_sub[i] * ks * dec)
            cols.append(jnp.concatenate(parts, axis=0).astype(BF16))
        att = jnp.dot(jnp.concatenate(cols, axis=1), esel_ref[...], preferred_element_type=F32)
        o = inter + jnp.dot(att[:, 0:chunk].astype(BF16), v.astype(BF16), preferred_element_type=F32)

        last = cum2[chunk - 1:chunk, :]
        kd = k * jnp.exp2(last - cum2)
        st_scr[...] = st * jnp.exp2(last) + jnp.dot(v.T.astype(BF16), kd.astype(BF16),
                                                     preferred_element_type=F32)
        gr = g_ref[pl.ds(r0, chunk), :]
        o_ref[pl.ds(r0, chunk), :] = _rms(o, ng_ref[...]) * (gr * jax.nn.sigmoid(gr))
        return carry

    lax.fori_loop(0, n_chunks, chunk_body, 0, unroll=2 if n_chunks % 2 == 0 else 1)

    @pl.when(ti == pl.num_programs(2) - 1)
    def _():
        sfin_ref[...] = st_scr[...]


def _hgrn(projf, lb, norm_gain, s0t, chunk, tt, n_valid):
    b, t, _ = projf.shape
    dk = HG_DK
    kern = functools.partial(_hgrn_kernel, chunk=chunk, n_chunks=tt // chunk, n_valid=n_valid)
    col = lambda j: pl.BlockSpec((None, tt, dk), lambda bi, h, ti: (bi, ti, j * HG_HEADS + h))
    vec = pl.BlockSpec((1, dk), lambda bi, h, ti: (0, h))
    st_spec = pl.BlockSpec((None, None, dk, dk), lambda bi, h, ti: (bi, h, 0, 0))
    esel = (jnp.arange(chunk * dk)[:, None] // dk == jnp.arange(LANES)[None, :]).astype(BF16)
    return pl.pallas_call(
        kern,
        grid=(b, HG_HEADS, t // tt),
        in_specs=[col(0), col(1), col(2), col(3), vec, vec, st_spec,
                  pl.BlockSpec((chunk * dk, LANES), lambda bi, h, ti: (0, 0))],
        out_specs=[pl.BlockSpec((None, tt, dk), lambda bi, h, ti: (bi, ti, h)), st_spec],
        out_shape=[jax.ShapeDtypeStruct((b, t, HG_HEADS * dk), F32),
                   jax.ShapeDtypeStruct((b, HG_HEADS, dk, dk), F32)],
        scratch_shapes=[pltpu.VMEM((dk, dk), F32)] + [pltpu.VMEM((chunk, dk), F32)] * 2,
        compiler_params=_cparams(("parallel", "parallel", "arbitrary")),
    )(projf, projf, projf, projf, lb.reshape(1, -1), norm_gain.reshape(1, -1), s0t, esel)


def _row_tile(n, pref):
    return math.gcd(n, pref)


def _pad_rows(x, rows):
    return jnp.pad(x, ((0, 0), (0, rows - x.shape[1]), (0, 0)))


def _even_weights(w_in, qk_a, qk_b, pe, w1, w2):
    w_pad = jnp.pad(w_in, ((0, 0), (0, EVEN_IN_PAD - EVEN_IN))).astype(BF16)
    ones = jnp.ones((HEAD_DIM,), F32)
    segs = [qk_a[0]] * 8 + [qk_a[1]] * 8 + [ones] * 8 + [qk_b[0]] * 8 + [ones] * 4 + [qk_b[2]] * 2 \
        + [ones] * 2 + [qk_b[3]] * 2 + [ones] * 4
    colgain = jnp.concatenate(segs).reshape(1, EVEN_IN_PAD).astype(F32)
    pe_cat = jnp.concatenate([pe[0], pe[0], pe[1], pe[1]], axis=-1).astype(F32)
    z1 = jnp.zeros_like(w1)
    w1bd = jnp.concatenate([jnp.concatenate([w1, z1], axis=-1), jnp.concatenate([z1, w1], axis=-1)],
                           axis=-2).astype(BF16)
    z2 = jnp.zeros_like(w2)
    w2bd = jnp.concatenate([jnp.concatenate([w2, z2], axis=-1), jnp.concatenate([z2, w2], axis=-1)],
                           axis=-2).astype(BF16)
    kgain = jnp.concatenate([qk_b[1], qk_b[1]]).reshape(1, LANES).astype(F32)
    pe_dl = jnp.swapaxes(pe, 1, 2)
    pe_dl = jnp.concatenate([pe_dl, pe_dl], axis=-1)
    pe_t = jnp.stack([pe_dl] * NSA_GROUPS, axis=1).reshape(2 * LANES, LANES).astype(F32)
    w1_dl = jnp.swapaxes(w1, 1, 2)
    z1t = jnp.zeros_like(w1_dl)
    w1t = jnp.concatenate([jnp.concatenate([w1_dl, z1t], axis=-1), jnp.concatenate([z1t, w1_dl], axis=-1)],
                          axis=-2).astype(BF16)
    return w_pad, colgain, (pe_cat, w1bd, w2bd, kgain), (pe_t, w1t, w2bd, kgain)


def kernel(x_prompt, x_sample, cache_diff_kv, cache_nsa_kv, cache_nsa_win, state_hgrn, page_table,
           norm_mix, norm_ffn, w_ffn_in, w_ffn_out, w_in_even, w_out_even, diff_qk_gain, diff_lambda,
           diff_subln_gain, nsa_qk_gain, nsa_cmp_pe, nsa_cmp_w1, nsa_cmp_w2, w_in_odd, w_out_odd,
           hgrn_norm_gain, hgrn_lb_logits):
    b, t, d = x_prompt.shape
    s, ts, _ = x_sample.shape
    depth = norm_mix.shape[0]
    n_even = cache_diff_kv.shape[0]
    n_pool, page_size = cache_diff_kv.shape[1], cache_diff_kv.shape[2]
    past_len = page_table.shape[1] * page_size
    n_buf = cache_nsa_win.shape[2]
    tsp = SAMPLE_ROWS
    assert ts <= tsp and d == D_MODEL

    lbw = jax.nn.softmax(hgrn_lb_logits.astype(F32), axis=0)
    lower_bounds = jnp.cumsum(lbw, axis=0) - lbw[0]

    hp = x_prompt.reshape(b * t, d)
    hs = _pad_rows(x_sample, tsp).reshape(s * tsp, d)
    tm_p = _row_tile(b * t, 512)
    tm_s = _row_tile(s * tsp, 256)
    cache_diff4 = cache_diff_kv.reshape(n_even, n_pool, page_size * 2 * DA_HEADS, DA_VDIM)
    cache_nsa_t = jnp.transpose(cache_nsa_kv, (0, 1, 3, 4, 5, 2)).reshape(
        n_even, n_pool, 4 * NSA_GROUPS * HEAD_DIM, page_size)
    cache_win = cache_nsa_win.reshape(n_even, s, n_buf, 2 * NSA_GROUPS * HEAD_DIM)
    live = (jnp.arange(tsp) < ts)[None, :, None]

    dkv_p, dkv_s, nkv_p, nkv_s, win_p, win_s, hg_p, hg_s = [], [], [], [], [], [], [], []
    for l in range(depth):
        if l % 2 == 0:
            e = l // 2
            lam_init = 0.8 - 0.6 * math.exp(-0.3 * l)
            w_pad, colgain, cw, cw_t = _even_weights(w_in_even[e], diff_qk_gain[e], nsa_qk_gain[e],
                                               nsa_cmp_pe[e], nsa_cmp_w1[e], nsa_cmp_w2[e])
            w_out = w_out_even[e].astype(BF16)
            pf, pb = _norm_proj(hp, norm_mix[l], w_pad, colgain, EVEN_NORM_BLOCKS, tm_p, True)
            pf3, pb3 = pf.reshape(b, t, -1), pb.reshape(b, t, -1)
            feat_major = lambda c0, c1: jnp.swapaxes(pb3[:, :, c0:c1], 1, 2)
            da = _diff_prompt(pb3, feat_major(C_DV, C_NQ), diff_lambda[e], diff_subln_gain[e], lam_init,
                              tq=_row_tile(t, 256), tk=_row_tile(t, 512))
            kcb, vcb = _compress_prompt(pf3, cw)
            onsa = _nsa_prompt_t(pf3, pb3, feat_major(C_NKV + 3 * LANES, C_NKV + 4 * LANES),
                                 feat_major(C_NKV + 5 * LANES, C_NKV + 6 * LANES), kcb, vcb,
                                 tq=_row_tile(t, 128), tk=_row_tile(t, 256))
            hp = _out_proj(da.reshape(b * t, -1), 0, onsa.reshape(b * t, -1), 0, hp, w_out, tm_p)
            dkv_p.append(pf3[:, :, C_DK:C_NQ].reshape(b, t, 2, DA_HEADS, DA_VDIM))
            nkv_p.append(pf3[:, :, C_NKV:C_NKV + 512].reshape(b, t, 4, NSA_GROUPS, HEAD_DIM))
            nw = min(NSA_WINDOW, t)
            win_p.append(pf3[:, t - nw:, C_NKV + 512:C_NG].reshape(b, nw, 2, NSA_GROUPS, HEAD_DIM))
            sf, sb = _norm_proj(hs, norm_mix[l], w_pad, colgain, EVEN_NORM_BLOCKS, tm_s, True)
            sf3, sb3 = sf.reshape(s, tsp, -1), sb.reshape(s, tsp, -1)
            diff_tail = _pad_rows(sf3[:, :, C_DK:C_NQ], LANES)
            da_s = _diff_sample(sb3, diff_tail, cache_diff4, e, page_table, diff_lambda[e],
                                diff_subln_gain[e], lam_init, past_len)
            as_page = lambda x: jnp.swapaxes(_pad_rows(x, page_size), 1, 2)
            cmp_tail = as_page(jnp.where(live, sf3[:, :, C_NKV:C_NKV + 256], 0.0))
            kcb_s, vcb_s = _compress_sample(cmp_tail, cache_nsa_t, e, page_table, cw_t)
            sel_tail = as_page(sf3[:, :, C_NKV + 256:C_NKV + 512])
            new_win = sf3[:, :, C_NKV + 512:C_NG]
            kwv = jnp.concatenate([cache_win[e], _pad_rows(new_win, page_size)], axis=1)
            onsa_s = _nsa_sample_t(sf3, sb3, kcb_s, vcb_s, sel_tail, kwv, cache_nsa_t, e, page_table,
                                   past_len, ts)
            hs = _out_proj(da_s.reshape(s * tsp, -1), 0, onsa_s.reshape(s * tsp, -1), 0, hs, w_out, tm_s)
            dkv_s.append(sf3[:, :ts, C_DK:C_NQ].reshape(s, ts, 2, DA_HEADS, DA_VDIM))
            nkv_s.append(sf3[:, :ts, C_NKV:C_NKV + 512].reshape(s, ts, 4, NSA_GROUPS, HEAD_DIM))
            win_all = jnp.concatenate([cache_win[e], new_win[:, :ts]], axis=1)[:, -n_buf:]
            win_s.append(win_all.reshape(s, n_buf, 2, NSA_GROUPS, HEAD_DIM))
        else:
            r = l // 2
            w_in = w_in_odd[r].astype(BF16)
            w_out = w_out_odd[r].astype(BF16)
            ones = jnp.ones((1, ODD_IN), F32)
            pf, = _norm_proj(hp, norm_mix[l], w_in, ones, (), tm_p, False)
            chunk = math.gcd(t, HG_CHUNK)
            o_p, st_p = _hgrn(pf.reshape(b, t, -1), lower_bounds[l], hgrn_norm_gain[r],
                              jnp.zeros((b, HG_HEADS, HG_DK, HG_DK), F32), chunk, _row_tile(t, 512), None)
            hp = _out_proj(o_p.reshape(b * t, -1), 0, o_p.reshape(b * t, -1), 1, hp, w_out, tm_p)
            hg_p.append(jnp.swapaxes(st_p, -1, -2))
            sf, = _norm_proj(hs, norm_mix[l], w_in, ones, (), tm_s, False)
            o_s, st_s = _hgrn(sf.reshape(s, tsp, -1), lower_bounds[l], hgrn_norm_gain[r],
                              jnp.swapaxes(state_hgrn[r].astype(F32), -1, -2), tsp, tsp, ts)
            hs = _out_proj(o_s.reshape(s * tsp, -1), 0, o_s.reshape(s * tsp, -1), 1, hs, w_out, tm_s)
            hg_s.append(jnp.swapaxes(st_s, -1, -2))
        w_fi = w_ffn_in[l].astype(BF16)
        w_fo = w_ffn_out[l].astype(BF16)
        tf = w_fo.shape[0] // 2
        hp = _ffn(hp, norm_ffn[l], w_fi, w_fo, tm_p, tf)
        hs = _ffn(hs, norm_ffn[l], w_fi, w_fo, tm_s, tf)

    y_s = hs.reshape(s, tsp, d)[:, :ts]
    return (hp.reshape(b, t, d), y_s, jnp.stack(dkv_p), jnp.stack(dkv_s), jnp.stack(nkv_p), jnp.stack(nkv_s),
            jnp.stack(win_p), jnp.stack(win_s), jnp.stack(hg_p), jnp.stack(hg_s))
```

```python
import functools
import math

import jax
import jax.numpy as jnp
from jax import lax
from jax.experimental import pallas as pl
from jax.experimental.pallas import tpu as pltpu

F32 = jnp.float32
BF16 = jnp.bfloat16

D_MODEL = 1024
HEAD_DIM = 64
DA_HEADS = 4
DA_VDIM = 2 * HEAD_DIM
NSA_HEADS = 8
NSA_GROUPS = 2
NSA_REP = NSA_HEADS // NSA_GROUPS
NSA_BLOCK = 64
NSA_TOPK = 16
NSA_WINDOW = 512
NSA_CMP_HID = 2 * HEAD_DIM
HG_HEADS = 8
HG_DK = D_MODEL // HG_HEADS
HG_CHUNK = 64
EPS = 1e-6
NEG = -1e30
M_FLOOR = -1e29
LOG2E = 1.4426950408889634

LANES = 128
SUBLANES = 8
VMEM_LIMIT = 56 * 1024 * 1024

C_DQ, C_DK, C_DV, C_NQ, C_NKV, C_NG = 0, 512, 1024, 1536, 2048, 2816
EVEN_IN = 2840
EVEN_IN_PAD = 2944
EVEN_NORM_BLOCKS = tuple(range(0, 8)) + tuple(range(12, 16)) + (18, 20)
ODD_IN = 4096
SAMPLE_ROWS = 16


def _cparams(sem):
    return pltpu.CompilerParams(dimension_semantics=sem, vmem_limit_bytes=VMEM_LIMIT)


def _nt_dot(a, b):
    return lax.dot_general(a, b, (((1,), (1,)), ((), ())), preferred_element_type=F32)


def _rms(x, gain):
    return x * lax.rsqrt(jnp.mean(x * x, axis=-1, keepdims=True) + EPS) * gain


def _seg_rms(y, gain):
    lo = lax.broadcasted_iota(jnp.int32, y.shape, 1) < HEAD_DIM
    y2 = y * y
    s_lo = jnp.sum(jnp.where(lo, y2, 0.0), axis=-1, keepdims=True)
    s_hi = jnp.sum(jnp.where(lo, 0.0, y2), axis=-1, keepdims=True)
    ms = jnp.where(lo, s_lo, s_hi) * (1.0 / HEAD_DIM)
    return y * lax.rsqrt(ms + EPS) * gain


def _norm_proj_kernel(x_ref, g_ref, w_ref, cg_ref, o_ref, *maybe_ob_ref, norm_blocks, col_chunk):
    xn = _rms(x_ref[...], g_ref[...]).astype(BF16)
    n_cols = w_ref.shape[1]
    for c0 in range(0, n_cols, col_chunk):
        c1 = min(c0 + col_chunk, n_cols)
        y = jnp.dot(xn, w_ref[:, c0:c1], preferred_element_type=F32)
        for b in range(c0 // LANES, c1 // LANES):
            yb = y[:, b * LANES - c0:(b + 1) * LANES - c0]
            if b in norm_blocks:
                yb = _seg_rms(yb, cg_ref[:, b * LANES:(b + 1) * LANES])
            o_ref[:, b * LANES:(b + 1) * LANES] = yb
            for ob_ref in maybe_ob_ref:
                ob_ref[:, b * LANES:(b + 1) * LANES] = yb.astype(BF16)


def _norm_proj(x2d, gain, w_bf16, colgain, norm_blocks, tm, with_bf16):
    n, d = x2d.shape
    c = w_bf16.shape[1]
    kern = functools.partial(_norm_proj_kernel, norm_blocks=norm_blocks, col_chunk=512)
    n_out = 2 if with_bf16 else 1
    return pl.pallas_call(
        kern,
        grid=(n // tm,),
        in_specs=[
            pl.BlockSpec((tm, d), lambda i: (i, 0)),
            pl.BlockSpec((1, d), lambda i: (0, 0)),
            pl.BlockSpec((d, c), lambda i: (0, 0)),
            pl.BlockSpec((1, c), lambda i: (0, 0)),
        ],
        out_specs=[pl.BlockSpec((tm, c), lambda i: (i, 0))] * n_out,
        out_shape=[jax.ShapeDtypeStruct((n, c), F32), jax.ShapeDtypeStruct((n, c), BF16)][:n_out],
        compiler_params=_cparams(("parallel",)),
    )(x2d, gain.reshape(1, d), w_bf16, colgain)


def _ffn_kernel(x_ref, g_ref, wg_ref, wu_ref, wo_ref, o_ref, xn_scr, acc_scr):
    f = pl.program_id(1)

    @pl.when(f == 0)
    def _():
        xn_scr[...] = _rms(x_ref[...], g_ref[...]).astype(BF16)
        acc_scr[...] = jnp.zeros_like(acc_scr)

    xn = xn_scr[...]
    gate = jnp.dot(xn, wg_ref[...], preferred_element_type=F32)
    up = jnp.dot(xn, wu_ref[...], preferred_element_type=F32)
    act = (gate * jax.nn.sigmoid(gate) * up).astype(BF16)
    acc_scr[...] += jnp.dot(act, wo_ref[...], preferred_element_type=F32)

    @pl.when(f == pl.num_programs(1) - 1)
    def _():
        o_ref[...] = x_ref[...] + acc_scr[...]


def _ffn(x2d, gain, w_in_bf16, w_out_bf16, tm, tf):
    n, d = x2d.shape
    dff = w_out_bf16.shape[0]
    nf = dff // tf
    return pl.pallas_call(
        _ffn_kernel,
        grid=(n // tm, nf),
        in_specs=[
            pl.BlockSpec((tm, d), lambda i, f: (i, 0)),
            pl.BlockSpec((1, d), lambda i, f: (0, 0)),
            pl.BlockSpec((d, tf), lambda i, f: (0, f)),
            pl.BlockSpec((d, tf), lambda i, f: (0, nf + f)),
            pl.BlockSpec((tf, d), lambda i, f: (f, 0)),
        ],
        out_specs=pl.BlockSpec((tm, d), lambda i, f: (i, 0)),
        out_shape=jax.ShapeDtypeStruct((n, d), F32),
        scratch_shapes=[pltpu.VMEM((tm, d), BF16), pltpu.VMEM((tm, d), F32)],
        compiler_params=_cparams(("parallel", "arbitrary")),
    )(x2d, gain.reshape(1, d), w_in_bf16, w_in_bf16, w_out_bf16)


def _out_proj_kernel(a_ref, b_ref, r_ref, w_ref, o_ref):
    half = a_ref.shape[1]
    y = jnp.dot(a_ref[...].astype(BF16), w_ref[0:half, :], preferred_element_type=F32)
    y += jnp.dot(b_ref[...].astype(BF16), w_ref[half:2 * half, :], preferred_element_type=F32)
    o_ref[...] = r_ref[...] + y


def _out_proj(a, a_blk, b, b_blk, resid, w_bf16, tm):
    n, d = resid.shape
    half = w_bf16.shape[0] // 2
    return pl.pallas_call(
        _out_proj_kernel,
        grid=(n // tm,),
        in_specs=[
            pl.BlockSpec((tm, half), lambda i: (i, a_blk)),
            pl.BlockSpec((tm, half), lambda i: (i, b_blk)),
            pl.BlockSpec((tm, d), lambda i: (i, 0)),
            pl.BlockSpec((2 * half, d), lambda i: (0, 0)),
        ],
        out_specs=pl.BlockSpec((tm, d), lambda i: (i, 0)),
        out_shape=jax.ShapeDtypeStruct((n, d), F32),
        compiler_params=_cparams(("parallel",)),
    )(a, b, resid, w_bf16)


def _flash_init(m_scr, l_scr, acc_scr):
    m_scr[...] = jnp.full(m_scr.shape, NEG, F32)
    l_scr[...] = jnp.zeros(l_scr.shape, F32)
    acc_scr[...] = jnp.zeros(acc_scr.shape, F32)


def _flash_update(s, mask, v_tile, m_scr, l_scr, acc_scr):
    s = jnp.where(mask, s, NEG)
    m_old = m_scr[...]
    m_new = jnp.maximum(m_old, jnp.max(s, axis=-1, keepdims=True))
    alpha = jnp.exp(m_old - m_new)
    p = jnp.where(mask, jnp.exp(s - m_new), 0.0)
    l_scr[...] = alpha * l_scr[...] + jnp.sum(p, axis=-1, keepdims=True)
    acc_scr[...] = alpha * acc_scr[...] + jnp.dot(p.astype(BF16), v_tile, preferred_element_type=F32)
    m_scr[...] = m_new


def _diff_lambda(lp_ref, lam_init):
    lp = lp_ref[...].astype(F32)
    a = jnp.sum(lp[0:1, :] * lp[1:2, :], axis=-1, keepdims=True)
    b = jnp.sum(lp[2:3, :] * lp[3:4, :], axis=-1, keepdims=True)
    return jnp.exp(a) - jnp.exp(b) + lam_init


def _flash_init_t(m_scr, l_scr, acc_scr):
    m_scr[...] = jnp.full(m_scr.shape, M_FLOOR, F32)
    l_scr[...] = jnp.zeros(l_scr.shape, F32)
    acc_scr[...] = jnp.zeros(acc_scr.shape, F32)


def _flash_update_t(st, vt_tile, m_scr, l_scr, acc_scr):
    m_old = m_scr[...]
    m_new = jnp.maximum(m_old, jnp.max(st, axis=0, keepdims=True))
    alpha = jnp.exp2(m_old - m_new)
    p = jnp.exp2(st - m_new)
    l_scr[...] = alpha * l_scr[...] + jnp.sum(p, axis=0, keepdims=True)
    acc_scr[...] = alpha * acc_scr[...] + jnp.dot(vt_tile, p.astype(BF16), preferred_element_type=F32)
    m_scr[...] = m_new


def _diff_qstack(q, scale):
    lo = lax.broadcasted_iota(jnp.int32, q.shape, 1) < HEAD_DIM
    qs = q.astype(F32) * scale
    return jnp.concatenate([jnp.where(lo, qs, 0.0), jnp.where(lo, 0.0, qs)], axis=0)


def _diff_finish(m_scr, l_scr, acc_scr, lam, gain, lam_init, tq):
    o = acc_scr[...] / jnp.maximum(l_scr[...], 1e-30)
    d = o[0:tq] - lam * o[tq:2 * tq]
    return _rms(d, gain) * (1.0 - lam_init)


def _diff_prompt_kernel(q_ref, k_ref, vt_ref, lp_ref, sg_ref, o_ref, m_scr, l_scr, acc_scr,
                        *, tq, tk, lam_init):
    assert tq & (tq - 1) == 0
    q0 = pl.program_id(2) * tq
    qst = _diff_qstack(q_ref[...], HEAD_DIM ** -0.5 * LOG2E).T.astype(BF16)
    lane = lax.broadcasted_iota(jnp.int32, (1, 2 * tq), 1)
    qpos = q0 + (lane & (tq - 1))
    _flash_init_t(m_scr, l_scr, acc_scr)

    def tile(ki, causal):
        k0 = pl.multiple_of(ki * tk, tk)
        st = jnp.dot(k_ref[pl.ds(k0, tk), :], qst, preferred_element_type=F32)
        if causal:
            kpos = k0 + lax.broadcasted_iota(jnp.int32, (tk, 1), 0)
            st = jnp.where(kpos <= qpos, st, NEG)
        _flash_update_t(st, vt_ref[:, pl.ds(k0, tk)], m_scr, l_scr, acc_scr)

    def full_tile(ki, carry):
        tile(ki, False)
        return carry

    def diag_tile(ki, carry):
        tile(ki, True)
        return carry

    n_full = q0 // tk
    lax.fori_loop(0, n_full, full_tile, 0)
    lax.fori_loop(n_full, (q0 + tq - 1) // tk + 1, diag_tile, 0)
    lam = _diff_lambda(lp_ref, lam_init)
    ot = acc_scr[...] / jnp.maximum(l_scr[...], 1e-30)
    dt = ot[:, 0:tq] - lam * ot[:, tq:2 * tq]
    dn = dt * lax.rsqrt(jnp.mean(dt * dt, axis=0, keepdims=True) + EPS)
    o_ref[...] = dn.T * sg_ref[...] * (1.0 - lam_init)


def _diff_prompt(projb, vt, lam_par, subln, lam_init, tq, tk):
    b, t, _ = projb.shape
    kern = functools.partial(_diff_prompt_kernel, tq=tq, tk=tk, lam_init=lam_init)
    kblk = C_DK // LANES
    return pl.pallas_call(
        kern,
        grid=(b, DA_HEADS, t // tq),
        in_specs=[
            pl.BlockSpec((None, tq, LANES), lambda bi, h, qi: (bi, qi, h)),
            pl.BlockSpec((None, t, LANES), lambda bi, h, qi: (bi, 0, kblk + h)),
            pl.BlockSpec((None, DA_VDIM, t), lambda bi, h, qi: (bi, h, 0)),
            pl.BlockSpec((4, HEAD_DIM), lambda bi, h, qi: (0, 0)),
            pl.BlockSpec((1, DA_VDIM), lambda bi, h, qi: (0, 0)),
        ],
        out_specs=pl.BlockSpec((None, tq, LANES), lambda bi, h, qi: (bi, qi, h)),
        out_shape=jax.ShapeDtypeStruct((b, t, DA_HEADS * DA_VDIM), F32),
        scratch_shapes=[pltpu.VMEM((1, 2 * tq), F32), pltpu.VMEM((1, 2 * tq), F32),
                        pltpu.VMEM((DA_VDIM, 2 * tq), F32)],
        compiler_params=_cparams(("parallel", "parallel", "arbitrary")),
    )(projb, projb, vt, lam_par, subln.reshape(1, DA_VDIM))


def _page_copies(cache_ref, e, pt_ref, seq, page0, n_pages, lane0, n_lanes, buf, sem, page_size):
    copies = []
    for j in range(n_pages):
        page = pt_ref[seq, page0 + j]
        copies.append(pltpu.make_async_copy(
            cache_ref.at[e, page, :, pl.ds(lane0, n_lanes)],
            buf.at[pl.ds(j * page_size, page_size), :],
            sem.at[j]))
    return copies


def _diff_sample_kernel(pt_ref, q_ref, tail_ref, lp_ref, sg_ref, cache_ref, o_ref,
                        buf0, buf1, sem0, sem1, m_scr, l_scr, acc_scr,
                        *, e, n_chunks, pages_per_chunk, page_size, past_len, tk, lam_init):
    s_idx = pl.program_id(0)
    c_idx = pl.program_id(1)
    bufs, sems = (buf0, buf1), (sem0, sem1)
    tq = SAMPLE_ROWS
    width = DA_HEADS * DA_VDIM
    assert tq & (tq - 1) == 0
    n_rows = DA_HEADS * 2 * tq
    row = lax.broadcasted_iota(jnp.int32, (n_rows, 1), 0)
    qpos = past_len + (row & (tq - 1))

    @pl.when(c_idx == 0)
    def _():
        _flash_init(m_scr, l_scr, acc_scr)

    qs = [_diff_qstack(q_ref[:, h * LANES:(h + 1) * LANES], HEAD_DIM ** -0.5).astype(BF16)
          for h in range(DA_HEADS)]

    def update(k_tile, v_tile, kpos):
        s = jnp.concatenate([_nt_dot(qs[h], k_tile(h).astype(BF16)) for h in range(DA_HEADS)], axis=0)
        mask = kpos <= qpos
        s = jnp.where(mask, s, NEG)
        m_old = m_scr[...]
        m_new = jnp.maximum(m_old, jnp.max(s, axis=-1, keepdims=True))
        alpha = jnp.exp(m_old - m_new)
        p = jnp.where(mask, jnp.exp(s - m_new), 0.0)
        l_scr[...] = alpha * l_scr[...] + jnp.sum(p, axis=-1, keepdims=True)
        pb = p.astype(BF16)
        pv = jnp.concatenate([jnp.dot(pb[h * 2 * tq:(h + 1) * 2 * tq], v_tile(h).astype(BF16),
                                      preferred_element_type=F32) for h in range(DA_HEADS)], axis=0)
        acc_scr[...] = alpha * acc_scr[...] + pv
        m_scr[...] = m_new

    @pl.when(c_idx < n_chunks)
    def _():
        per_pos = 2 * DA_HEADS
        n_seq = pl.num_programs(0)
        chunk_rows = pages_per_chunk * page_size

        def chunk_copies(seq, chunk, slot):
            return _page_copies(cache_ref, e, pt_ref, seq, chunk * pages_per_chunk, pages_per_chunk,
                                0, LANES, bufs[slot], sems[slot], page_size * per_pos)

        def run(slot):
            @pl.when((s_idx == 0) & (c_idx == 0))
            def _():
                for cp in chunk_copies(s_idx, c_idx, slot):
                    cp.start()

            @pl.when(c_idx + 1 < n_chunks)
            def _():
                for cp in chunk_copies(s_idx, c_idx + 1, 1 - slot):
                    cp.start()

            @pl.when((c_idx + 1 == n_chunks) & (s_idx + 1 < n_seq))
            def _():
                for cp in chunk_copies(s_idx + 1, 0, 1 - slot):
                    cp.start()

            for cp in chunk_copies(s_idx, c_idx, slot):
                cp.wait()
            buf = bufs[slot]
            for r0 in range(0, chunk_rows, tk):
                kpos = (c_idx * chunk_rows + r0) + lax.broadcasted_iota(jnp.int32, (1, tk), 1)
                update(lambda h: buf[pl.ds(r0 * per_pos + h, tk, stride=per_pos), :],
                       lambda h: buf[pl.ds(r0 * per_pos + DA_HEADS + h, tk, stride=per_pos), :], kpos)

        parity = (s_idx * n_chunks + c_idx) & 1
        for slot in range(2):
            pl.when(parity == slot)(functools.partial(run, slot))

    @pl.when(c_idx == n_chunks)
    def _():
        n_tail = tail_ref.shape[0]
        kpos = past_len + lax.broadcasted_iota(jnp.int32, (1, n_tail), 1)
        lam = _diff_lambda(lp_ref, lam_init)
        update(lambda h: tail_ref[:, h * LANES:(h + 1) * LANES],
               lambda h: tail_ref[:, width + h * LANES:width + (h + 1) * LANES], kpos)
        for h in range(DA_HEADS):
            hs = slice(h * 2 * tq, (h + 1) * 2 * tq)
            o_ref[:, h * LANES:(h + 1) * LANES] = _diff_finish(
                m_scr.at[hs], l_scr.at[hs], acc_scr.at[hs], lam, sg_ref[...], lam_init, tq)


def _diff_sample(projb_s, tail, cache4, e, page_table, lam_par, subln, lam_init, past_len):
    s, tq, _ = projb_s.shape
    n_pages = page_table.shape[1]
    page_size = cache4.shape[2] // (2 * DA_HEADS)
    pages_per_chunk = math.gcd(n_pages, 16)
    n_chunks = n_pages // pages_per_chunk
    width = DA_HEADS * DA_VDIM
    tk = math.gcd(pages_per_chunk * page_size, 512)
    kern = functools.partial(_diff_sample_kernel, e=e, n_chunks=n_chunks, pages_per_chunk=pages_per_chunk,
                             page_size=page_size, past_len=past_len, tk=tk, lam_init=lam_init)
    grid_spec = pltpu.PrefetchScalarGridSpec(
        num_scalar_prefetch=1,
        grid=(s, n_chunks + 1),
        in_specs=[
            pl.BlockSpec((None, tq, width), lambda si, ci, pt: (si, 0, 0)),
            pl.BlockSpec((None, tail.shape[1], 2 * width), lambda si, ci, pt: (si, 0, 0)),
            pl.BlockSpec((4, HEAD_DIM), lambda si, ci, pt: (0, 0)),
            pl.BlockSpec((1, DA_VDIM), lambda si, ci, pt: (0, 0)),
            pl.BlockSpec(memory_space=pl.ANY),
        ],
        out_specs=pl.BlockSpec((None, tq, width), lambda si, ci, pt: (si, 0, 0)),
        scratch_shapes=[
            pltpu.VMEM((pages_per_chunk * page_size * 2 * DA_HEADS, LANES), F32),
            pltpu.VMEM((pages_per_chunk * page_size * 2 * DA_HEADS, LANES), F32),
            pltpu.SemaphoreType.DMA((pages_per_chunk,)),
            pltpu.SemaphoreType.DMA((pages_per_chunk,)),
            pltpu.VMEM((DA_HEADS * 2 * tq, 1), F32),
            pltpu.VMEM((DA_HEADS * 2 * tq, 1), F32),
            pltpu.VMEM((DA_HEADS * 2 * tq, LANES), F32),
        ],
    )
    return pl.pallas_call(
        kern,
        grid_spec=grid_spec,
        out_shape=jax.ShapeDtypeStruct((s, tq, width), F32),
        compiler_params=_cparams(("arbitrary", "arbitrary")),
    )(page_table, projb_s, tail, lam_par, subln.reshape(1, DA_VDIM), cache4)


def _compress_compute(xk_ref, xv_ref, pe_ref, w1_ref, w2_ref, kg_ref, kcb_ref, vcb_ref, nb):
    def body(l, accs):
        pe = pe_ref[pl.ds(l, 1), :]
        ak = (xk_ref[pl.ds(l, nb, stride=NSA_BLOCK), :] + pe[:, 0:LANES]).astype(BF16)
        av = (xv_ref[pl.ds(l, nb, stride=NSA_BLOCK), :] + pe[:, LANES:2 * LANES]).astype(BF16)
        hk = jnp.dot(ak, w1_ref[0, l], preferred_element_type=F32)
        hv = jnp.dot(av, w1_ref[1, l], preferred_element_type=F32)
        return accs[0] + hk, accs[1] + hv

    zero = jnp.zeros((nb, 2 * NSA_CMP_HID), F32)
    hk, hv = lax.fori_loop(0, NSA_BLOCK, body, (zero, zero))
    ck = jnp.dot(jax.nn.gelu(hk).astype(BF16), w2_ref[0], preferred_element_type=F32)
    cv = jnp.dot(jax.nn.gelu(hv).astype(BF16), w2_ref[1], preferred_element_type=F32)
    ck = _seg_rms(ck, kg_ref[...])
    for ref in (kcb_ref, vcb_ref):
        ref[...] = jnp.zeros(ref.shape, ref.dtype)
    kcb_ref[0:nb, :] = ck.astype(BF16)
    vcb_ref[0:nb, :] = cv.astype(BF16)


def _compress_prompt_kernel(xk_ref, xv_ref, pe_ref, w1_ref, w2_ref, kg_ref, kcb_ref, vcb_ref, *, nb):
    _compress_compute(xk_ref, xv_ref, pe_ref, w1_ref, w2_ref, kg_ref, kcb_ref, vcb_ref, nb)


def _compress_specs(nbp):
    hid2 = 2 * NSA_CMP_HID
    w_specs = [
        pl.BlockSpec((NSA_BLOCK, 2 * LANES), lambda *a: (0, 0)),
        pl.BlockSpec((2, NSA_BLOCK, LANES, hid2), lambda *a: (0, 0, 0, 0)),
        pl.BlockSpec((2, hid2, LANES), lambda *a: (0, 0, 0)),
        pl.BlockSpec((1, LANES), lambda *a: (0, 0)),
    ]
    out_specs = [pl.BlockSpec((None, nbp, LANES), lambda bi, *a: (bi, 0, 0))] * 2
    return w_specs, out_specs


def _compress_out_shape(nbat, nbp):
    return [jax.ShapeDtypeStruct((nbat, nbp, LANES), BF16)] * 2


def _compress_prompt(projf, cw):
    b, t, _ = projf.shape
    nb = t // NSA_BLOCK
    nbp = -(-nb // LANES) * LANES
    w_specs, out_specs = _compress_specs(nbp)
    return pl.pallas_call(
        functools.partial(_compress_prompt_kernel, nb=nb),
        grid=(b,),
        in_specs=[pl.BlockSpec((None, t, LANES), lambda bi: (bi, 0, C_NKV // LANES)),
                  pl.BlockSpec((None, t, LANES), lambda bi: (bi, 0, C_NKV // LANES + 1))] + w_specs,
        out_specs=out_specs,
        out_shape=_compress_out_shape(b, nbp),
        compiler_params=_cparams(("parallel",)),
    )(projf, projf, *cw)


def _feature_page_copies(cache_ref, e, pt_ref, seq, n_pages, row0, n_rows, buf, sem):
    copies = []
    for j in range(n_pages):
        copies.append(pltpu.make_async_copy(
            cache_ref.at[e, pt_ref[seq, j], pl.ds(row0, n_rows), :],
            buf.at[pl.ds(j * n_rows, n_rows), :],
            sem.at[j]))
    return copies


def _per_sequence_pages(cache_ref, e, pt_ref, n_pages, row0, n_rows, bufs, sems, use):
    s_idx = pl.program_id(0)

    def run(slot):
        def copies(seq, which):
            return _feature_page_copies(cache_ref, e, pt_ref, seq, n_pages, row0, n_rows, bufs[which], sems[which])

        @pl.when(s_idx == 0)
        def _():
            for cp in copies(s_idx, slot):
                cp.start()

        @pl.when(s_idx + 1 < pl.num_programs(0))
        def _():
            for cp in copies(s_idx + 1, 1 - slot):
                cp.start()

        for cp in copies(s_idx, slot):
            cp.wait()
        use(bufs[slot])

    for slot in range(2):
        pl.when((s_idx & 1) == slot)(functools.partial(run, slot))


def _compress_sample_kernel(pt_ref, tail_ref, pe_ref, w1_ref, w2_ref, kg_ref, cache_ref,
                            kcb_ref, vcb_ref, buf0, buf1, sem0, sem1, *, e, n_pages, half_stride):
    feat = 2 * LANES
    compute = functools.partial(_compress_pages, tail_ref, pe_ref, w1_ref, w2_ref, kg_ref, kcb_ref, vcb_ref,
                                n_pages=n_pages, half_stride=half_stride)
    _per_sequence_pages(cache_ref, e, pt_ref, n_pages, 0, feat, (buf0, buf1), (sem0, sem1), compute)


def _compress_pages(tail_ref, pe_ref, w1_ref, w2_ref, kg_ref, kcb_ref, vcb_ref, buf, *, n_pages, half_stride):
    feat = 2 * LANES
    buf[n_pages * feat:, :] = tail_ref[...]
    n_pp = n_pages + 1

    def body(d, accs):
        out = []
        for c in range(2):
            for g in range(NSA_GROUPS):
                f = c * LANES + g * HEAD_DIM + d
                a = (buf[pl.ds(f, n_pp, stride=feat), :] + pe_ref[pl.ds(f, 1), :]).astype(BF16)
                out.append(accs[c * NSA_GROUPS + g] + jnp.dot(a, w1_ref[c, d], preferred_element_type=F32))
        return tuple(out)

    zero = jnp.zeros((n_pp, 2 * NSA_CMP_HID), F32)
    accs = lax.fori_loop(0, HEAD_DIM, body, (zero,) * (2 * NSA_GROUPS))
    pad = jnp.zeros((-n_pp % (2 * SUBLANES), LANES), F32)
    for c, out_ref in ((0, kcb_ref), (1, vcb_ref)):
        out_ref[...] = jnp.zeros(out_ref.shape, out_ref.dtype)
        for half in range(2):
            hs = slice(half * NSA_CMP_HID, (half + 1) * NSA_CMP_HID)
            hid = jnp.concatenate([accs[c * NSA_GROUPS + g][:, hs] for g in range(NSA_GROUPS)], axis=1)
            y = jnp.dot(jax.nn.gelu(hid).astype(BF16), w2_ref[c], preferred_element_type=F32)
            if c == 0:
                y = _seg_rms(y, kg_ref[...])
            y = jnp.concatenate([y, pad], axis=0).astype(BF16)
            out_ref[half * half_stride:half * half_stride + y.shape[0], :] = y


def _compress_sample(tail_t, cache_t, e, page_table, cw_t):
    s = tail_t.shape[0]
    n_pages = page_table.shape[1]
    page_size = cache_t.shape[3]
    assert page_size == 2 * NSA_BLOCK
    feat = 2 * LANES
    half_stride = -(-(n_pages + 1) // LANES) * LANES
    nbp = 2 * half_stride
    hid2 = 2 * NSA_CMP_HID
    grid_spec = pltpu.PrefetchScalarGridSpec(
        num_scalar_prefetch=1,
        grid=(s,),
        in_specs=[
            pl.BlockSpec((None, feat, page_size), lambda si, pt: (si, 0, 0)),
            pl.BlockSpec((feat, page_size), lambda si, pt: (0, 0)),
            pl.BlockSpec((2, HEAD_DIM, LANES, hid2), lambda si, pt: (0, 0, 0, 0)),
            pl.BlockSpec((2, hid2, LANES), lambda si, pt: (0, 0, 0)),
            pl.BlockSpec((1, LANES), lambda si, pt: (0, 0)),
            pl.BlockSpec(memory_space=pl.ANY),
        ],
        out_specs=[pl.BlockSpec((None, nbp, LANES), lambda si, pt: (si, 0, 0))] * 2,
        scratch_shapes=[pltpu.VMEM(((n_pages + 1) * feat, page_size), F32)] * 2
        + [pltpu.SemaphoreType.DMA((n_pages,))] * 2,
    )
    return pl.pallas_call(
        functools.partial(_compress_sample_kernel, e=e, n_pages=n_pages, half_stride=half_stride),
        grid_spec=grid_spec,
        out_shape=_compress_out_shape(s, nbp),
        compiler_params=_cparams(("arbitrary",)),
    )(page_table, tail_t, *cw_t, cache_t)


def _nsa_branches(q, kc_ref, vc_ref, k_sel_tile, vt_sel_tile, k_win_tile, vt_win_tile,
                  sel_scr, m_s, l_s, a_s, m_w, l_w, a_w,
                  *, tq, q0, n_blk, n_sel_rows, tk, wbase, n_win_rows, tkw,
                  row_block=lambda row: row, block_row=lambda blk: blk):
    nh = NSA_HEADS
    n_rows = nh * tq
    blk_shift = NSA_BLOCK.bit_length() - 1
    tq_shift = tq.bit_length() - 1
    assert 1 << tq_shift == tq and 1 << blk_shift == NSA_BLOCK
    lane128 = lax.broadcasted_iota(jnp.int32, (tq, LANES), 1)
    qf = q.astype(F32) * (HEAD_DIM ** -0.5 * LOG2E)
    parts = []
    for hh in range(nh):
        g = hh // NSA_REP
        blk = qf[:, (hh // 2) * LANES:(hh // 2 + 1) * LANES]
        if hh % 2 != g:
            blk = pltpu.roll(blk, HEAD_DIM, 1)
        in_seg = (lane128 >= g * HEAD_DIM) & (lane128 < (g + 1) * HEAD_DIM)
        parts.append(jnp.where(in_seg, blk, 0.0))
    qst = jnp.concatenate(parts, axis=0).T.astype(BF16)

    lane = lax.broadcasted_iota(jnp.int32, (1, n_rows), 1)
    qpos = q0 + (lane & (tq - 1))

    nbp = kc_ref.shape[0]
    nb8 = sel_scr.shape[0]
    assert n_blk <= nb8 <= nbp
    sc = jnp.dot(kc_ref[...], qst, preferred_element_type=F32)
    blk_p = row_block(lax.broadcasted_iota(jnp.int32, (nbp, 1), 0))
    cmp_ok = blk_p < ((qpos + 1) >> blk_shift)
    sc = jnp.where(cmp_ok, sc, NEG)
    mc = jnp.max(sc, axis=0, keepdims=True)
    ec = jnp.where(cmp_ok, jnp.exp2(sc - mc), 0.0)
    pc_all = ec / jnp.maximum(jnp.sum(ec, axis=0, keepdims=True), 1e-30)
    vct = vc_ref[...].astype(F32).T.astype(BF16)
    o_cmp = jnp.dot(vct, pc_all.astype(BF16), preferred_element_type=F32)
    pc = pc_all[0:nb8]
    blk_i = blk_p[0:nb8]
    blk_f = blk_i.astype(F32)

    n_pick = min(NSA_TOPK - 1, n_blk)

    def pick(imp, cur):
        cand = blk_i < cur
        x = jnp.where(cand, imp, -1.0)
        x = jnp.where(blk_i < n_blk, x, -2.0)
        picked = jnp.zeros(x.shape, F32)
        for _ in range(n_pick):
            mx = jnp.max(x, axis=0, keepdims=True)
            first = jnp.min(jnp.where(x == mx, blk_f, 3e38), axis=0, keepdims=True)
            hit = blk_f == first
            picked = jnp.where(hit, 1.0, picked)
            x = jnp.where(hit, -3.0, x)
        chosen = (blk_i == cur) | (cand & (picked > 0.5))
        return jnp.where(chosen, 0.0, NEG)

    if tq % LANES == 0:
        cur_t = (q0 + lax.broadcasted_iota(jnp.int32, (1, tq), 1)) >> blk_shift
        for g in range(NSA_GROUPS):
            base = g * NSA_REP * tq
            imp = pc[:, base:base + tq]
            for r in range(1, NSA_REP):
                imp = imp + pc[:, base + r * tq:base + (r + 1) * tq]
            sel_g = pick(imp, cur_t)
            for r in range(NSA_REP):
                sel_scr[:, base + r * tq:base + (r + 1) * tq] = sel_g
    else:
        assert n_rows == LANES
        rep = (lane >> tq_shift) & (NSA_REP - 1)
        imp = pc
        for d in range(1, NSA_REP):
            up = pltpu.roll(pc, LANES - d * tq, 1)
            dn = pltpu.roll(pc, d * tq, 1)
            imp = imp + jnp.where(rep + d < NSA_REP, up, 0.0) + jnp.where(rep >= d, dn, 0.0)
        sel_scr[...] = pick(imp, qpos >> blk_shift)

    _flash_init_t(m_s, l_s, a_s)
    blocks_per_tile = tk // NSA_BLOCK

    def sel_tile(ki, causal):
        k0 = pl.multiple_of(ki * tk, tk)
        st = jnp.dot(k_sel_tile(k0), qst, preferred_element_type=F32)
        b0 = ki * blocks_per_tile
        kpos = k0 + lax.broadcasted_iota(jnp.int32, (tk, 1), 0)
        pieces = []
        for bb in range(blocks_per_tile):
            rs = slice(bb * NSA_BLOCK, (bb + 1) * NSA_BLOCK)
            piece = st[rs] + sel_scr[pl.ds(block_row(b0 + bb), 1), :]
            pieces.append(jnp.where(kpos[rs] <= qpos, piece, NEG) if causal else piece)
        _flash_update_t(jnp.concatenate(pieces, axis=0), vt_sel_tile(k0), m_s, l_s, a_s)

    def sel_full(ki, carry):
        sel_tile(ki, False)
        return carry

    def sel_diag(ki, carry):
        sel_tile(ki, True)
        return carry

    n_sel = jnp.minimum((q0 + tq - 1) // tk, n_sel_rows // tk - 1) + 1
    n_full = jnp.minimum(q0 // tk, n_sel)
    lax.fori_loop(0, n_full, sel_full, 0)
    lax.fori_loop(n_full, n_sel, sel_diag, 0)

    _flash_init_t(m_w, l_w, a_w)

    def win_body(ki, carry):
        k0 = pl.multiple_of(ki * tkw, tkw)
        st = jnp.dot(k_win_tile(k0), qst, preferred_element_type=F32)
        kpos = wbase + k0 + lax.broadcasted_iota(jnp.int32, (tkw, 1), 0)
        kpos = jnp.where(kpos >= 0, kpos, jnp.iinfo(jnp.int32).max)
        dist = qpos - kpos
        ok = (dist >= 0) & (dist <= NSA_WINDOW)
        _flash_update_t(jnp.where(ok, st, NEG), vt_win_tile(k0), m_w, l_w, a_w)
        return carry

    lo_tile = jnp.maximum(q0 - NSA_WINDOW - wbase, 0) // tkw
    hi_tile = jnp.minimum((q0 + tq - 1 - wbase) // tkw, n_win_rows // tkw - 1)
    lax.fori_loop(lo_tile, hi_tile + 1, win_body, 0)

    o_sel = a_s[...] / jnp.maximum(l_s[...], 1e-30)
    o_win = a_w[...] / jnp.maximum(l_w[...], 1e-30)
    return o_cmp, o_sel, o_win


def _nsa_gate_lanes(o_cmp, o_sel, o_win, gate_logits, tq):
    gt = jax.nn.sigmoid(gate_logits).T
    def gate_row(branch):
        return jnp.concatenate([gt[3 * hh + branch:3 * hh + branch + 1, :] for hh in range(NSA_HEADS)], axis=1)
    ot = gate_row(0) * o_cmp + gate_row(1) * o_sel + gate_row(2) * o_win
    outs = []
    for j in range(NSA_HEADS // 2):
        g = (2 * j) // NSA_REP
        rs = slice(g * HEAD_DIM, (g + 1) * HEAD_DIM)
        pair = jnp.concatenate([ot[rs, (2 * j) * tq:(2 * j + 1) * tq],
                                ot[rs, (2 * j + 1) * tq:(2 * j + 2) * tq]], axis=0)
        outs.append(pair.T)
    return jnp.concatenate(outs, axis=1)


def _nsa_gate_rows(o_cmp, o_sel, o_win, gate_logits, tq):
    oc, os_, ow = o_cmp.T, o_sel.T, o_win.T
    gates = jax.nn.sigmoid(gate_logits)
    lo_half = lax.broadcasted_iota(jnp.int32, (tq, LANES), 1) < HEAD_DIM
    heads = []
    for hh in range(NSA_HEADS):
        g = hh // NSA_REP
        rs = slice(hh * tq, (hh + 1) * tq)
        o = (gates[:, 3 * hh:3 * hh + 1] * oc[rs] + gates[:, 3 * hh + 1:3 * hh + 2] * os_[rs]
             + gates[:, 3 * hh + 2:3 * hh + 3] * ow[rs])
        if hh % 2 != g:
            o = pltpu.roll(o, HEAD_DIM, 1)
        heads.append(o)
    return jnp.concatenate([jnp.where(lo_half, heads[2 * j], heads[2 * j + 1])
                            for j in range(NSA_HEADS // 2)], axis=1)


def _nsa_scratch_t(tq, nb8):
    n_rows = NSA_HEADS * tq
    one = [pltpu.VMEM((1, n_rows), F32), pltpu.VMEM((1, n_rows), F32), pltpu.VMEM((LANES, n_rows), F32)]
    return [pltpu.VMEM((nb8, n_rows), F32)] + one + one


def _nsa_prompt_kernel_t(q_ref, g_ref, kc_ref, vc_ref, ks_ref, vst_ref, kw_ref, vwt_ref, o_ref,
                         sel_scr, m_s, l_s, a_s, m_w, l_w, a_w, *, tq, tk, tkw, n_blk, t):
    q0 = pl.program_id(1) * tq
    branches = _nsa_branches(
        q_ref[...], kc_ref, vc_ref,
        lambda k0: ks_ref[pl.ds(k0, tk), :], lambda k0: vst_ref[:, pl.ds(k0, tk)],
        lambda k0: kw_ref[pl.ds(k0, tkw), :], lambda k0: vwt_ref[:, pl.ds(k0, tkw)],
        sel_scr, m_s, l_s, a_s, m_w, l_w, a_w,
        tq=tq, q0=q0, n_blk=n_blk, n_sel_rows=t, tk=tk, wbase=0, n_win_rows=t, tkw=tkw)
    o_ref[...] = _nsa_gate_lanes(*branches, g_ref[...], tq)


def _nsa_prompt_t(projf, projb, vst, vwt, kcb, vcb, tq, tk):
    b, t, _ = projb.shape
    nbp = kcb.shape[1]
    n_blk = t // NSA_BLOCK
    nb8 = -(-n_blk // (2 * SUBLANES)) * (2 * SUBLANES)
    tkw = min(tq, LANES)
    kern = functools.partial(_nsa_prompt_kernel_t, tq=tq, tk=tk, tkw=tkw, n_blk=n_blk, t=t)
    ks_blk = (C_NKV + 2 * LANES) // LANES
    kw_blk = (C_NKV + 4 * LANES) // LANES
    return pl.pallas_call(
        kern,
        grid=(b, t // tq),
        in_specs=[
            pl.BlockSpec((None, tq, 512), lambda bi, qi: (bi, qi, C_NQ // 512)),
            pl.BlockSpec((None, tq, LANES), lambda bi, qi: (bi, qi, C_NG // LANES)),
            pl.BlockSpec((None, nbp, LANES), lambda bi, qi: (bi, 0, 0)),
            pl.BlockSpec((None, nbp, LANES), lambda bi, qi: (bi, 0, 0)),
            pl.BlockSpec((None, t, LANES), lambda bi, qi: (bi, 0, ks_blk)),
            pl.BlockSpec((None, LANES, t), lambda bi, qi: (bi, 0, 0)),
            pl.BlockSpec((None, t, LANES), lambda bi, qi: (bi, 0, kw_blk)),
            pl.BlockSpec((None, LANES, t), lambda bi, qi: (bi, 0, 0)),
        ],
        out_specs=pl.BlockSpec((None, tq, 512), lambda bi, qi: (bi, qi, 0)),
        out_shape=jax.ShapeDtypeStruct((b, t, NSA_HEADS * HEAD_DIM), F32),
        scratch_shapes=_nsa_scratch_t(tq, nb8),
        compiler_params=_cparams(("parallel", "arbitrary")),
    )(projb, projf, kcb, vcb, projb, vst, projb, vwt)


def _nsa_sample_kernel_t(pt_ref, q_ref, g_ref, kc_ref, vc_ref, tail_ref, kwv_ref, cache_ref, o_ref,
                         buf0, buf1, sem0, sem1, sel_scr, m_s, l_s, a_s, m_w, l_w, a_w,
                         *, e, n_pages, page_size, past_len, n_blk, tkw, wbase, half_stride, pages_per_tile):
    feat = 2 * LANES
    compute = functools.partial(
        _nsa_sample_pages, q_ref, g_ref, kc_ref, vc_ref, tail_ref, kwv_ref, o_ref,
        sel_scr, m_s, l_s, a_s, m_w, l_w, a_w, n_pages=n_pages, page_size=page_size, past_len=past_len,
        n_blk=n_blk, tkw=tkw, wbase=wbase, half_stride=half_stride, pages_per_tile=pages_per_tile)
    _per_sequence_pages(cache_ref, e, pt_ref, n_pages, feat, feat, (buf0, buf1), (sem0, sem1), compute)


def _nsa_sample_pages(q_ref, g_ref, kc_ref, vc_ref, tail_ref, kwv_ref, o_ref,
                      sel_scr, m_s, l_s, a_s, m_w, l_w, a_w, buf,
                      *, n_pages, page_size, past_len, n_blk, tkw, wbase, half_stride, pages_per_tile):
    feat = 2 * LANES
    buf[n_pages * feat:, :] = tail_ref[...]
    tq = q_ref.shape[0]
    tk = pages_per_tile * page_size

    def page_rows(k0, j, first):
        return pl.ds(pl.multiple_of((k0 // page_size + j) * feat + first, LANES), LANES)

    def k_tile(k0):
        return jnp.concatenate([buf[page_rows(k0, j, 0), :].T for j in range(pages_per_tile)],
                               axis=0).astype(BF16)

    def vt_tile(k0):
        return jnp.concatenate([buf[page_rows(k0, j, LANES), :] for j in range(pages_per_tile)],
                               axis=1).astype(BF16)

    def row_block(row):
        half = (row >= half_stride).astype(jnp.int32)
        page = row - half * half_stride
        return jnp.where(page <= n_pages, 2 * page + half, jnp.iinfo(jnp.int32).max)

    branches = _nsa_branches(
        q_ref[...], kc_ref, vc_ref,
        k_tile, vt_tile,
        lambda k0: kwv_ref[pl.ds(k0, tkw), 0:LANES].astype(BF16),
        lambda k0: kwv_ref[pl.ds(k0, tkw), LANES:2 * LANES].T.astype(BF16),
        sel_scr, m_s, l_s, a_s, m_w, l_w, a_w,
        tq=tq, q0=past_len, n_blk=n_blk, n_sel_rows=(n_pages + 1) * page_size, tk=tk,
        wbase=wbase, n_win_rows=kwv_ref.shape[0], tkw=tkw,
        row_block=row_block, block_row=lambda blk: (blk & 1) * half_stride + (blk >> 1))
    o_ref[...] = _nsa_gate_rows(*branches, g_ref[...], tq)


def _nsa_sample_t(projf_s, projb_s, kcb, vcb, sel_tail_t, kwv, cache_t, e, page_table, past_len, n_valid):
    s, tq, _ = projb_s.shape
    n_pages = page_table.shape[1]
    page_size = cache_t.shape[3]
    feat = 2 * LANES
    nbp = kcb.shape[1]
    half_stride = nbp // 2
    n_blk = -(-(past_len + n_valid) // NSA_BLOCK)
    nb8 = nbp
    tkw = math.gcd(kwv.shape[1], LANES)
    wbase = past_len - (kwv.shape[1] - page_size)
    assert page_size == LANES
    pages_per_tile = max(p for p in range(1, 9) if (n_pages + 1) % p == 0)
    kern = functools.partial(_nsa_sample_kernel_t, e=e, n_pages=n_pages, page_size=page_size,
                             past_len=past_len, n_blk=n_blk, tkw=tkw, wbase=wbase, half_stride=half_stride,
                             pages_per_tile=pages_per_tile)
    grid_spec = pltpu.PrefetchScalarGridSpec(
        num_scalar_prefetch=1,
        grid=(s,),
        in_specs=[
            pl.BlockSpec((None, tq, 512), lambda si, pt: (si, 0, C_NQ // 512)),
            pl.BlockSpec((None, tq, LANES), lambda si, pt: (si, 0, C_NG // LANES)),
            pl.BlockSpec((None, nbp, LANES), lambda si, pt: (si, 0, 0)),
            pl.BlockSpec((None, nbp, LANES), lambda si, pt: (si, 0, 0)),
            pl.BlockSpec((None, feat, page_size), lambda si, pt: (si, 0, 0)),
            pl.BlockSpec((None, kwv.shape[1], 2 * LANES), lambda si, pt: (si, 0, 0)),
            pl.BlockSpec(memory_space=pl.ANY),
        ],
        out_specs=pl.BlockSpec((None, tq, 512), lambda si, pt: (si, 0, 0)),
        scratch_shapes=[pltpu.VMEM(((n_pages + 1) * feat, page_size), F32)] * 2
        + [pltpu.SemaphoreType.DMA((n_pages,))] * 2 + _nsa_scratch_t(tq, nb8),
    )
    return pl.pallas_call(
        kern,
        grid_spec=grid_spec,
        out_shape=jax.ShapeDtypeStruct((s, tq, NSA_HEADS * HEAD_DIM), F32),
        compiler_params=_cparams(("arbitrary",)),
    )(page_table, projb_s, projf_s, kcb, vcb, sel_tail_t, kwv, cache_t)


def _split3(x):
    hi = x.astype(BF16)
    r1 = x - hi.astype(F32)
    mid = r1.astype(BF16)
    lo = (r1 - mid.astype(F32)).astype(BF16)
    return hi, mid, lo


def _hgrn_kernel(q_ref, f_ref, i_ref, g_ref, lb_ref, ng_ref, s0_ref, o_ref, sfin_ref,
                 st_scr, cum_scr, k_scr, v_scr, *, chunk, n_chunks, n_valid):
    ti = pl.program_id(2)

    @pl.when(ti == 0)
    def _():
        st_scr[...] = s0_ref[...]

    lb = lb_ref[...]
    tri = (lax.broadcasted_iota(jnp.int32, (chunk, chunk), 0)
           >= lax.broadcasted_iota(jnp.int32, (chunk, chunk), 1)).astype(BF16)
    row = lax.broadcasted_iota(jnp.int32, (chunk, 1), 0)
    row8 = lax.broadcasted_iota(jnp.int32, (SUBLANES, 1), 0)
    n_sub = chunk // SUBLANES

    def chunk_body(c, carry):
        r0 = pl.multiple_of(c * chunk, chunk)
        qr = q_ref[pl.ds(r0, chunk), :]
        q = qr * jax.nn.sigmoid(qr) * (HG_DK ** -0.5)
        fg = lb + (1.0 - lb) * jax.nn.sigmoid(f_ref[pl.ds(r0, chunk), :])
        logf = jnp.log(fg)
        k = 1.0 - fg
        v = i_ref[pl.ds(r0, chunk), :]
        if n_valid is not None:
            live = (ti * (chunk * n_chunks) + r0 + row) < n_valid
            logf = jnp.where(live, logf, 0.0)
            k = jnp.where(live, k, 0.0)
        cum = sum(jnp.dot(tri, part, preferred_element_type=F32) for part in _split3(logf))
        cum2 = cum * LOG2E
        cum_scr[...] = cum2
        k_scr[...] = k
        v_scr[...] = v
        st = st_scr[...]
        inter = _nt_dot((q * jnp.exp2(cum2)).astype(BF16), st.astype(BF16))

        q_sub = [q[i * SUBLANES:(i + 1) * SUBLANES] for i in range(n_sub)]
        c_sub = [cum2[i * SUBLANES:(i + 1) * SUBLANES] for i in range(n_sub)]
        prods = []
        for s in range(chunk):
            j = s // SUBLANES
            cs = jnp.broadcast_to(cum_scr[s:s + 1, :], (SUBLANES, HG_DK))
            ks = jnp.broadcast_to(k_scr[s:s + 1, :], (SUBLANES, HG_DK))
            for i in range(j, n_sub):
                dec = jnp.exp2(c_sub[i] - cs)
                if i == j:
                    dec = jnp.where(row8 >= s - j * SUBLANES, dec, 0.0)
                prods.append(q_sub[i] * ks * dec)
        sums = jnp.dot(jnp.concatenate(prods, axis=0).astype(BF16), jnp.ones((HG_DK, HG_DK), BF16),
                       preferred_element_type=F32)
        o_sub = [inter[i * SUBLANES:(i + 1) * SUBLANES] for i in range(n_sub)]
        n = 0
        for s in range(chunk):
            vs = jnp.broadcast_to(v_scr[s:s + 1, :], (SUBLANES, HG_DK))
            for i in range(s // SUBLANES, n_sub):
                o_sub[i] = o_sub[i] + sums[n * SUBLANES:(n + 1) * SUBLANES] * vs
                n += 1
        o = jnp.concatenate(o_sub, axis=0)

        last = cum2[chunk - 1:chunk, :]
        kd = k * jnp.exp2(last - cum2)
        st_scr[...] = st * jnp.exp2(last) + jnp.dot(v.T.astype(BF16), kd.astype(BF16),
                                                     preferred_element_type=F32)
        gr = g_ref[pl.ds(r0, chunk), :]
        o_ref[pl.ds(r0, chunk), :] = _rms(o, ng_ref[...]) * (gr * jax.nn.sigmoid(gr))
        return carry

    lax.fori_loop(0, n_chunks, chunk_body, 0, unroll=2 if n_chunks % 2 == 0 else 1)

    @pl.when(ti == pl.num_programs(2) - 1)
    def _():
        sfin_ref[...] = st_scr[...]


def _hgrn(projf, lb, norm_gain, s0t, chunk, tt, n_valid):
    b, t, _ = projf.shape
    dk = HG_DK
    kern = functools.partial(_hgrn_kernel, chunk=chunk, n_chunks=tt // chunk, n_valid=n_valid)
    col = lambda j: pl.BlockSpec((None, tt, dk), lambda bi, h, ti: (bi, ti, j * HG_HEADS + h))
    vec = pl.BlockSpec((1, dk), lambda bi, h, ti: (0, h))
    st_spec = pl.BlockSpec((None, None, dk, dk), lambda bi, h, ti: (bi, h, 0, 0))
    return pl.pallas_call(
        kern,
        grid=(b, HG_HEADS, t // tt),
        in_specs=[col(0), col(1), col(2), col(3), vec, vec, st_spec],
        out_specs=[pl.BlockSpec((None, tt, dk), lambda bi, h, ti: (bi, ti, h)), st_spec],
        out_shape=[jax.ShapeDtypeStruct((b, t, HG_HEADS * dk), F32),
                   jax.ShapeDtypeStruct((b, HG_HEADS, dk, dk), F32)],
        scratch_shapes=[pltpu.VMEM((dk, dk), F32)] + [pltpu.VMEM((chunk, dk), F32)] * 3,
        compiler_params=_cparams(("parallel", "parallel", "arbitrary")),
    )(projf, projf, projf, projf, lb.reshape(1, -1), norm_gain.reshape(1, -1), s0t)


def _row_tile(n, pref):
    return math.gcd(n, pref)


def _pad_rows(x, rows):
    return jnp.pad(x, ((0, 0), (0, rows - x.shape[1]), (0, 0)))


def _even_weights(w_in, qk_a, qk_b, pe, w1, w2):
    w_pad = jnp.pad(w_in, ((0, 0), (0, EVEN_IN_PAD - EVEN_IN))).astype(BF16)
    ones = jnp.ones((HEAD_DIM,), F32)
    segs = [qk_a[0]] * 8 + [qk_a[1]] * 8 + [ones] * 8 + [qk_b[0]] * 8 + [ones] * 4 + [qk_b[2]] * 2 \
        + [ones] * 2 + [qk_b[3]] * 2 + [ones] * 4
    colgain = jnp.concatenate(segs).reshape(1, EVEN_IN_PAD).astype(F32)
    pe_cat = jnp.concatenate([pe[0], pe[0], pe[1], pe[1]], axis=-1).astype(F32)
    z1 = jnp.zeros_like(w1)
    w1bd = jnp.concatenate([jnp.concatenate([w1, z1], axis=-1), jnp.concatenate([z1, w1], axis=-1)],
                           axis=-2).astype(BF16)
    z2 = jnp.zeros_like(w2)
    w2bd = jnp.concatenate([jnp.concatenate([w2, z2], axis=-1), jnp.concatenate([z2, w2], axis=-1)],
                           axis=-2).astype(BF16)
    kgain = jnp.concatenate([qk_b[1], qk_b[1]]).reshape(1, LANES).astype(F32)
    pe_dl = jnp.swapaxes(pe, 1, 2)
    pe_dl = jnp.concatenate([pe_dl, pe_dl], axis=-1)
    pe_t = jnp.stack([pe_dl] * NSA_GROUPS, axis=1).reshape(2 * LANES, LANES).astype(F32)
    w1_dl = jnp.swapaxes(w1, 1, 2)
    z1t = jnp.zeros_like(w1_dl)
    w1t = jnp.concatenate([jnp.concatenate([w1_dl, z1t], axis=-1), jnp.concatenate([z1t, w1_dl], axis=-1)],
                          axis=-2).astype(BF16)
    return w_pad, colgain, (pe_cat, w1bd, w2bd, kgain), (pe_t, w1t, w2bd, kgain)


def kernel(x_prompt, x_sample, cache_diff_kv, cache_nsa_kv, cache_nsa_win, state_hgrn, page_table,
           norm_mix, norm_ffn, w_ffn_in, w_ffn_out, w_in_even, w_out_even, diff_qk_gain, diff_lambda,
           diff_subln_gain, nsa_qk_gain, nsa_cmp_pe, nsa_cmp_w1, nsa_cmp_w2, w_in_odd, w_out_odd,
           hgrn_norm_gain, hgrn_lb_logits):
    b, t, d = x_prompt.shape
    s, ts, _ = x_sample.shape
    depth = norm_mix.shape[0]
    n_even = cache_diff_kv.shape[0]
    n_pool, page_size = cache_diff_kv.shape[1], cache_diff_kv.shape[2]
    past_len = page_table.shape[1] * page_size
    n_buf = cache_nsa_win.shape[2]
    tsp = SAMPLE_ROWS
    assert ts <= tsp and d == D_MODEL

    lbw = jax.nn.softmax(hgrn_lb_logits.astype(F32), axis=0)
    lower_bounds = jnp.cumsum(lbw, axis=0) - lbw[0]

    hp = x_prompt.reshape(b * t, d)
    hs = _pad_rows(x_sample, tsp).reshape(s * tsp, d)
    tm_p = _row_tile(b * t, 512)
    tm_s = _row_tile(s * tsp, 256)
    cache_diff4 = cache_diff_kv.reshape(n_even, n_pool, page_size * 2 * DA_HEADS, DA_VDIM)
    cache_nsa_t = jnp.transpose(cache_nsa_kv, (0, 1, 3, 4, 5, 2)).reshape(
        n_even, n_pool, 4 * NSA_GROUPS * HEAD_DIM, page_size)
    cache_win = cache_nsa_win.reshape(n_even, s, n_buf, 2 * NSA_GROUPS * HEAD_DIM)
    live = (jnp.arange(tsp) < ts)[None, :, None]

    dkv_p, dkv_s, nkv_p, nkv_s, win_p, win_s, hg_p, hg_s = [], [], [], [], [], [], [], []
    for l in range(depth):
        if l % 2 == 0:
            e = l // 2
            lam_init = 0.8 - 0.6 * math.exp(-0.3 * l)
            w_pad, colgain, cw, cw_t = _even_weights(w_in_even[e], diff_qk_gain[e], nsa_qk_gain[e],
                                               nsa_cmp_pe[e], nsa_cmp_w1[e], nsa_cmp_w2[e])
            w_out = w_out_even[e].astype(BF16)
            pf, pb = _norm_proj(hp, norm_mix[l], w_pad, colgain, EVEN_NORM_BLOCKS, tm_p, True)
            pf3, pb3 = pf.reshape(b, t, -1), pb.reshape(b, t, -1)
            feat_major = lambda c0, c1: jnp.swapaxes(pb3[:, :, c0:c1], 1, 2)
            da = _diff_prompt(pb3, feat_major(C_DV, C_NQ), diff_lambda[e], diff_subln_gain[e], lam_init,
                              tq=_row_tile(t, 256), tk=_row_tile(t, 512))
            kcb, vcb = _compress_prompt(pf3, cw)
            onsa = _nsa_prompt_t(pf3, pb3, feat_major(C_NKV + 3 * LANES, C_NKV + 4 * LANES),
                                 feat_major(C_NKV + 5 * LANES, C_NKV + 6 * LANES), kcb, vcb,
                                 tq=_row_tile(t, 128), tk=_row_tile(t, 256))
            hp = _out_proj(da.reshape(b * t, -1), 0, onsa.reshape(b * t, -1), 0, hp, w_out, tm_p)
            dkv_p.append(pf3[:, :, C_DK:C_NQ].reshape(b, t, 2, DA_HEADS, DA_VDIM))
            nkv_p.append(pf3[:, :, C_NKV:C_NKV + 512].reshape(b, t, 4, NSA_GROUPS, HEAD_DIM))
            nw = min(NSA_WINDOW, t)
            win_p.append(pf3[:, t - nw:, C_NKV + 512:C_NG].reshape(b, nw, 2, NSA_GROUPS, HEAD_DIM))
            sf, sb = _norm_proj(hs, norm_mix[l], w_pad, colgain, EVEN_NORM_BLOCKS, tm_s, True)
            sf3, sb3 = sf.reshape(s, tsp, -1), sb.reshape(s, tsp, -1)
            diff_tail = _pad_rows(sf3[:, :, C_DK:C_NQ], LANES)
            da_s = _diff_sample(sb3, diff_tail, cache_diff4, e, page_table, diff_lambda[e],
                                diff_subln_gain[e], lam_init, past_len)
            as_page = lambda x: jnp.swapaxes(_pad_rows(x, page_size), 1, 2)
            cmp_tail = as_page(jnp.where(live, sf3[:, :, C_NKV:C_NKV + 256], 0.0))
            kcb_s, vcb_s = _compress_sample(cmp_tail, cache_nsa_t, e, page_table, cw_t)
            sel_tail = as_page(sf3[:, :, C_NKV + 256:C_NKV + 512])
            new_win = sf3[:, :, C_NKV + 512:C_NG]
            kwv = jnp.concatenate([cache_win[e], _pad_rows(new_win, page_size)], axis=1)
            onsa_s = _nsa_sample_t(sf3, sb3, kcb_s, vcb_s, sel_tail, kwv, cache_nsa_t, e, page_table,
                                   past_len, ts)
            hs = _out_proj(da_s.reshape(s * tsp, -1), 0, onsa_s.reshape(s * tsp, -1), 0, hs, w_out, tm_s)
            dkv_s.append(sf3[:, :ts, C_DK:C_NQ].reshape(s, ts, 2, DA_HEADS, DA_VDIM))
            nkv_s.append(sf3[:, :ts, C_NKV:C_NKV + 512].reshape(s, ts, 4, NSA_GROUPS, HEAD_DIM))
            win_all = jnp.concatenate([cache_win[e], new_win[:, :ts]], axis=1)[:, -n_buf:]
            win_s.append(win_all.reshape(s, n_buf, 2, NSA_GROUPS, HEAD_DIM))
        else:
            r = l // 2
            w_in = w_in_odd[r].astype(BF16)
            w_out = w_out_odd[r].astype(BF16)
            ones = jnp.ones((1, ODD_IN), F32)
            pf, = _norm_proj(hp, norm_mix[l], w_in, ones, (), tm_p, False)
            chunk = math.gcd(t, HG_CHUNK)
            o_p, st_p = _hgrn(pf.reshape(b, t, -1), lower_bounds[l], hgrn_norm_gain[r],
                              jnp.zeros((b, HG_HEADS, HG_DK, HG_DK), F32), chunk, _row_tile(t, 512), None)
            hp = _out_proj(o_p.reshape(b * t, -1), 0, o_p.reshape(b * t, -1), 1, hp, w_out, tm_p)
            hg_p.append(jnp.swapaxes(st_p, -1, -2))
            sf, = _norm_proj(hs, norm_mix[l], w_in, ones, (), tm_s, False)
            o_s, st_s = _hgrn(sf.reshape(s, tsp, -1), lower_bounds[l], hgrn_norm_gain[r],
                              jnp.swapaxes(state_hgrn[r].astype(F32), -1, -2), tsp, tsp, ts)
            hs = _out_proj(o_s.reshape(s * tsp, -1), 0, o_s.reshape(s * tsp, -1), 1, hs, w_out, tm_s)
            hg_s.append(jnp.swapaxes(st_s, -1, -2))
        w_fi = w_ffn_in[l].astype(BF16)
        w_fo = w_ffn_out[l].astype(BF16)
        tf = w_fo.shape[0] // 2
        hp = _ffn(hp, norm_ffn[l], w_fi, w_fo, tm_p, tf)
        hs = _ffn(hs, norm_ffn[l], w_fi, w_fo, tm_s, tf)

    y_s = hs.reshape(s, tsp, d)[:, :ts]
    return (hp.reshape(b, t, d), y_s, jnp.stack(dkv_p), jnp.stack(dkv_s), jnp.stack(nkv_p), jnp.stack(nkv_s),
            jnp.stack(win_p), jnp.stack(win_s), jnp.stack(hg_p), jnp.stack(hg_s))
```

```python
import functools
import math

import jax
import jax.numpy as jnp
from jax import lax
from jax.experimental import pallas as pl
from jax.experimental.pallas import tpu as pltpu

F32 = jnp.float32
BF16 = jnp.bfloat16

D_MODEL = 1024
HEAD_DIM = 64
DA_HEADS = 4
DA_VDIM = 2 * HEAD_DIM
NSA_HEADS = 8
NSA_GROUPS = 2
NSA_REP = NSA_HEADS // NSA_GROUPS
NSA_BLOCK = 64
NSA_TOPK = 16
NSA_WINDOW = 512
NSA_CMP_HID = 2 * HEAD_DIM
HG_HEADS = 8
HG_DK = D_MODEL // HG_HEADS
HG_CHUNK = 64
EPS = 1e-6
NEG = -1e30
M_FLOOR = -1e29
LOG2E = 1.4426950408889634

LANES = 128
SUBLANES = 8
VMEM_LIMIT = 56 * 1024 * 1024

C_DQ, C_DK, C_DV, C_NQ, C_NKV, C_NG = 0, 512, 1024, 1536, 2048, 2816
EVEN_IN = 2840
EVEN_IN_PAD = 2944
EVEN_NORM_BLOCKS = tuple(range(0, 8)) + tuple(range(12, 16)) + (18, 20)
ODD_IN = 4096
SAMPLE_ROWS = 16


def _cparams(sem):
    return pltpu.CompilerParams(dimension_semantics=sem, vmem_limit_bytes=VMEM_LIMIT)


def _nt_dot(a, b):
    return lax.dot_general(a, b, (((1,), (1,)), ((), ())), preferred_element_type=F32)


def _rms(x, gain):
    return x * lax.rsqrt(jnp.mean(x * x, axis=-1, keepdims=True) + EPS) * gain


def _seg_rms(y, gain):
    lo = lax.broadcasted_iota(jnp.int32, y.shape, 1) < HEAD_DIM
    y2 = y * y
    s_lo = jnp.sum(jnp.where(lo, y2, 0.0), axis=-1, keepdims=True)
    s_hi = jnp.sum(jnp.where(lo, 0.0, y2), axis=-1, keepdims=True)
    ms = jnp.where(lo, s_lo, s_hi) * (1.0 / HEAD_DIM)
    return y * lax.rsqrt(ms + EPS) * gain


def _norm_proj_kernel(x_ref, g_ref, w_ref, cg_ref, o_ref, *maybe_ob_ref, norm_blocks, col_chunk):
    xn = _rms(x_ref[...], g_ref[...]).astype(BF16)
    n_cols = w_ref.shape[1]
    for c0 in range(0, n_cols, col_chunk):
        c1 = min(c0 + col_chunk, n_cols)
        y = jnp.dot(xn, w_ref[:, c0:c1], preferred_element_type=F32)
        for b in range(c0 // LANES, c1 // LANES):
            yb = y[:, b * LANES - c0:(b + 1) * LANES - c0]
            if b in norm_blocks:
                yb = _seg_rms(yb, cg_ref[:, b * LANES:(b + 1) * LANES])
            o_ref[:, b * LANES:(b + 1) * LANES] = yb
            for ob_ref in maybe_ob_ref:
                ob_ref[:, b * LANES:(b + 1) * LANES] = yb.astype(BF16)


def _norm_proj(x2d, gain, w_bf16, colgain, norm_blocks, tm, with_bf16):
    n, d = x2d.shape
    c = w_bf16.shape[1]
    kern = functools.partial(_norm_proj_kernel, norm_blocks=norm_blocks, col_chunk=512)
    n_out = 2 if with_bf16 else 1
    return pl.pallas_call(
        kern,
        grid=(n // tm,),
        in_specs=[
            pl.BlockSpec((tm, d), lambda i: (i, 0)),
            pl.BlockSpec((1, d), lambda i: (0, 0)),
            pl.BlockSpec((d, c), lambda i: (0, 0)),
            pl.BlockSpec((1, c), lambda i: (0, 0)),
        ],
        out_specs=[pl.BlockSpec((tm, c), lambda i: (i, 0))] * n_out,
        out_shape=[jax.ShapeDtypeStruct((n, c), F32), jax.ShapeDtypeStruct((n, c), BF16)][:n_out],
        compiler_params=_cparams(("parallel",)),
    )(x2d, gain.reshape(1, d), w_bf16, colgain)


def _ffn_kernel(x_ref, g_ref, wg_ref, wu_ref, wo_ref, o_ref, xn_scr, acc_scr):
    f = pl.program_id(1)

    @pl.when(f == 0)
    def _():
        xn_scr[...] = _rms(x_ref[...], g_ref[...]).astype(BF16)
        acc_scr[...] = jnp.zeros_like(acc_scr)

    xn = xn_scr[...]
    gate = jnp.dot(xn, wg_ref[...], preferred_element_type=F32)
    up = jnp.dot(xn, wu_ref[...], preferred_element_type=F32)
    act = (gate * jax.nn.sigmoid(gate) * up).astype(BF16)
    acc_scr[...] += jnp.dot(act, wo_ref[...], preferred_element_type=F32)

    @pl.when(f == pl.num_programs(1) - 1)
    def _():
        o_ref[...] = x_ref[...] + acc_scr[...]


def _ffn(x2d, gain, w_in_bf16, w_out_bf16, tm, tf):
    n, d = x2d.shape
    dff = w_out_bf16.shape[0]
    nf = dff // tf
    return pl.pallas_call(
        _ffn_kernel,
        grid=(n // tm, nf),
        in_specs=[
            pl.BlockSpec((tm, d), lambda i, f: (i, 0)),
            pl.BlockSpec((1, d), lambda i, f: (0, 0)),
            pl.BlockSpec((d, tf), lambda i, f: (0, f)),
            pl.BlockSpec((d, tf), lambda i, f: (0, nf + f)),
            pl.BlockSpec((tf, d), lambda i, f: (f, 0)),
        ],
        out_specs=pl.BlockSpec((tm, d), lambda i, f: (i, 0)),
        out_shape=jax.ShapeDtypeStruct((n, d), F32),
        scratch_shapes=[pltpu.VMEM((tm, d), BF16), pltpu.VMEM((tm, d), F32)],
        compiler_params=_cparams(("parallel", "arbitrary")),
    )(x2d, gain.reshape(1, d), w_in_bf16, w_in_bf16, w_out_bf16)


def _out_proj_kernel(a_ref, b_ref, r_ref, w_ref, o_ref):
    half = a_ref.shape[1]
    y = jnp.dot(a_ref[...].astype(BF16), w_ref[0:half, :], preferred_element_type=F32)
    y += jnp.dot(b_ref[...].astype(BF16), w_ref[half:2 * half, :], preferred_element_type=F32)
    o_ref[...] = r_ref[...] + y


def _out_proj(a, a_blk, b, b_blk, resid, w_bf16, tm):
    n, d = resid.shape
    half = w_bf16.shape[0] // 2
    return pl.pallas_call(
        _out_proj_kernel,
        grid=(n // tm,),
        in_specs=[
            pl.BlockSpec((tm, half), lambda i: (i, a_blk)),
            pl.BlockSpec((tm, half), lambda i: (i, b_blk)),
            pl.BlockSpec((tm, d), lambda i: (i, 0)),
            pl.BlockSpec((2 * half, d), lambda i: (0, 0)),
        ],
        out_specs=pl.BlockSpec((tm, d), lambda i: (i, 0)),
        out_shape=jax.ShapeDtypeStruct((n, d), F32),
        compiler_params=_cparams(("parallel",)),
    )(a, b, resid, w_bf16)


def _flash_init(m_scr, l_scr, acc_scr):
    m_scr[...] = jnp.full(m_scr.shape, NEG, F32)
    l_scr[...] = jnp.zeros(l_scr.shape, F32)
    acc_scr[...] = jnp.zeros(acc_scr.shape, F32)


def _flash_update(s, mask, v_tile, m_scr, l_scr, acc_scr):
    s = jnp.where(mask, s, NEG)
    m_old = m_scr[...]
    m_new = jnp.maximum(m_old, jnp.max(s, axis=-1, keepdims=True))
    alpha = jnp.exp(m_old - m_new)
    p = jnp.where(mask, jnp.exp(s - m_new), 0.0)
    l_scr[...] = alpha * l_scr[...] + jnp.sum(p, axis=-1, keepdims=True)
    acc_scr[...] = alpha * acc_scr[...] + jnp.dot(p.astype(BF16), v_tile, preferred_element_type=F32)
    m_scr[...] = m_new


def _diff_lambda(lp_ref, lam_init):
    lp = lp_ref[...].astype(F32)
    a = jnp.sum(lp[0:1, :] * lp[1:2, :], axis=-1, keepdims=True)
    b = jnp.sum(lp[2:3, :] * lp[3:4, :], axis=-1, keepdims=True)
    return jnp.exp(a) - jnp.exp(b) + lam_init


def _flash_init_t(m_scr, l_scr, acc_scr):
    m_scr[...] = jnp.full(m_scr.shape, M_FLOOR, F32)
    l_scr[...] = jnp.zeros(l_scr.shape, F32)
    acc_scr[...] = jnp.zeros(acc_scr.shape, F32)


def _flash_update_t(st, vt_tile, m_scr, l_scr, acc_scr):
    m_old = m_scr[...]
    m_new = jnp.maximum(m_old, jnp.max(st, axis=0, keepdims=True))
    alpha = jnp.exp2(m_old - m_new)
    p = jnp.exp2(st - m_new)
    l_scr[...] = alpha * l_scr[...] + jnp.sum(p, axis=0, keepdims=True)
    acc_scr[...] = alpha * acc_scr[...] + jnp.dot(vt_tile, p.astype(BF16), preferred_element_type=F32)
    m_scr[...] = m_new


def _diff_qstack(q, scale):
    lo = lax.broadcasted_iota(jnp.int32, q.shape, 1) < HEAD_DIM
    qs = q.astype(F32) * scale
    return jnp.concatenate([jnp.where(lo, qs, 0.0), jnp.where(lo, 0.0, qs)], axis=0)


def _diff_finish(m_scr, l_scr, acc_scr, lam, gain, lam_init, tq):
    o = acc_scr[...] / jnp.maximum(l_scr[...], 1e-30)
    d = o[0:tq] - lam * o[tq:2 * tq]
    return _rms(d, gain) * (1.0 - lam_init)


def _diff_prompt_kernel(q_ref, k_ref, vt_ref, lp_ref, sg_ref, o_ref, m_scr, l_scr, acc_scr,
                        *, tq, tk, lam_init):
    assert tq & (tq - 1) == 0
    q0 = pl.program_id(2) * tq
    qst = _diff_qstack(q_ref[...], HEAD_DIM ** -0.5 * LOG2E).T.astype(BF16)
    lane = lax.broadcasted_iota(jnp.int32, (1, 2 * tq), 1)
    qpos = q0 + (lane & (tq - 1))
    _flash_init_t(m_scr, l_scr, acc_scr)

    def tile(ki, causal):
        k0 = pl.multiple_of(ki * tk, tk)
        st = jnp.dot(k_ref[pl.ds(k0, tk), :], qst, preferred_element_type=F32)
        if causal:
            kpos = k0 + lax.broadcasted_iota(jnp.int32, (tk, 1), 0)
            st = jnp.where(kpos <= qpos, st, NEG)
        _flash_update_t(st, vt_ref[:, pl.ds(k0, tk)], m_scr, l_scr, acc_scr)

    def full_tile(ki, carry):
        tile(ki, False)
        return carry

    def diag_tile(ki, carry):
        tile(ki, True)
        return carry

    n_full = q0 // tk
    lax.fori_loop(0, n_full, full_tile, 0)
    lax.fori_loop(n_full, (q0 + tq - 1) // tk + 1, diag_tile, 0)
    lam = _diff_lambda(lp_ref, lam_init)
    ot = acc_scr[...] / jnp.maximum(l_scr[...], 1e-30)
    dt = ot[:, 0:tq] - lam * ot[:, tq:2 * tq]
    dn = dt * lax.rsqrt(jnp.mean(dt * dt, axis=0, keepdims=True) + EPS)
    o_ref[...] = dn.T * sg_ref[...] * (1.0 - lam_init)


def _diff_prompt(projb, vt, lam_par, subln, lam_init, tq, tk):
    b, t, _ = projb.shape
    kern = functools.partial(_diff_prompt_kernel, tq=tq, tk=tk, lam_init=lam_init)
    kblk = C_DK // LANES
    return pl.pallas_call(
        kern,
        grid=(b, DA_HEADS, t // tq),
        in_specs=[
            pl.BlockSpec((None, tq, LANES), lambda bi, h, qi: (bi, qi, h)),
            pl.BlockSpec((None, t, LANES), lambda bi, h, qi: (bi, 0, kblk + h)),
            pl.BlockSpec((None, DA_VDIM, t), lambda bi, h, qi: (bi, h, 0)),
            pl.BlockSpec((4, HEAD_DIM), lambda bi, h, qi: (0, 0)),
            pl.BlockSpec((1, DA_VDIM), lambda bi, h, qi: (0, 0)),
        ],
        out_specs=pl.BlockSpec((None, tq, LANES), lambda bi, h, qi: (bi, qi, h)),
        out_shape=jax.ShapeDtypeStruct((b, t, DA_HEADS * DA_VDIM), F32),
        scratch_shapes=[pltpu.VMEM((1, 2 * tq), F32), pltpu.VMEM((1, 2 * tq), F32),
                        pltpu.VMEM((DA_VDIM, 2 * tq), F32)],
        compiler_params=_cparams(("parallel", "parallel", "arbitrary")),
    )(projb, projb, vt, lam_par, subln.reshape(1, DA_VDIM))


def _page_copies(cache_ref, e, pt_ref, seq, page0, n_pages, lane0, n_lanes, buf, sem, page_size):
    copies = []
    for j in range(n_pages):
        page = pt_ref[seq, page0 + j]
        copies.append(pltpu.make_async_copy(
            cache_ref.at[e, page, :, pl.ds(lane0, n_lanes)],
            buf.at[pl.ds(j * page_size, page_size), :],
            sem.at[j]))
    return copies


def _diff_sample_kernel(pt_ref, q_ref, tail_ref, lp_ref, sg_ref, cache_ref, o_ref,
                        buf0, buf1, sem0, sem1, m_scr, l_scr, acc_scr,
                        *, e, n_chunks, pages_per_chunk, page_size, past_len, tk, lam_init):
    s_idx = pl.program_id(0)
    c_idx = pl.program_id(1)
    bufs, sems = (buf0, buf1), (sem0, sem1)
    tq = SAMPLE_ROWS
    width = DA_HEADS * DA_VDIM
    assert tq & (tq - 1) == 0
    n_rows = DA_HEADS * 2 * tq
    row = lax.broadcasted_iota(jnp.int32, (n_rows, 1), 0)
    qpos = past_len + (row & (tq - 1))

    @pl.when(c_idx == 0)
    def _():
        _flash_init(m_scr, l_scr, acc_scr)

    qs = [_diff_qstack(q_ref[:, h * LANES:(h + 1) * LANES], HEAD_DIM ** -0.5).astype(BF16)
          for h in range(DA_HEADS)]

    def update(k_tile, v_tile, kpos):
        s = jnp.concatenate([_nt_dot(qs[h], k_tile(h).astype(BF16)) for h in range(DA_HEADS)], axis=0)
        mask = kpos <= qpos
        s = jnp.where(mask, s, NEG)
        m_old = m_scr[...]
        m_new = jnp.maximum(m_old, jnp.max(s, axis=-1, keepdims=True))
        alpha = jnp.exp(m_old - m_new)
        p = jnp.where(mask, jnp.exp(s - m_new), 0.0)
        l_scr[...] = alpha * l_scr[...] + jnp.sum(p, axis=-1, keepdims=True)
        pb = p.astype(BF16)
        pv = jnp.concatenate([jnp.dot(pb[h * 2 * tq:(h + 1) * 2 * tq], v_tile(h).astype(BF16),
                                      preferred_element_type=F32) for h in range(DA_HEADS)], axis=0)
        acc_scr[...] = alpha * acc_scr[...] + pv
        m_scr[...] = m_new

    @pl.when(c_idx < n_chunks)
    def _():
        per_pos = 2 * DA_HEADS
        n_seq = pl.num_programs(0)
        chunk_rows = pages_per_chunk * page_size

        def chunk_copies(seq, chunk, slot):
            return _page_copies(cache_ref, e, pt_ref, seq, chunk * pages_per_chunk, pages_per_chunk,
                                0, LANES, bufs[slot], sems[slot], page_size * per_pos)

        def run(slot):
            @pl.when((s_idx == 0) & (c_idx == 0))
            def _():
                for cp in chunk_copies(s_idx, c_idx, slot):
                    cp.start()

            @pl.when(c_idx + 1 < n_chunks)
            def _():
                for cp in chunk_copies(s_idx, c_idx + 1, 1 - slot):
                    cp.start()

            @pl.when((c_idx + 1 == n_chunks) & (s_idx + 1 < n_seq))
            def _():
                for cp in chunk_copies(s_idx + 1, 0, 1 - slot):
                    cp.start()

            for cp in chunk_copies(s_idx, c_idx, slot):
                cp.wait()
            buf = bufs[slot]
            for r0 in range(0, chunk_rows, tk):
                kpos = (c_idx * chunk_rows + r0) + lax.broadcasted_iota(jnp.int32, (1, tk), 1)
                update(lambda h: buf[pl.ds(r0 * per_pos + h, tk, stride=per_pos), :],
                       lambda h: buf[pl.ds(r0 * per_pos + DA_HEADS + h, tk, stride=per_pos), :], kpos)

        parity = (s_idx * n_chunks + c_idx) & 1
        for slot in range(2):
            pl.when(parity == slot)(functools.partial(run, slot))

    @pl.when(c_idx == n_chunks)
    def _():
        n_tail = tail_ref.shape[0]
        kpos = past_len + lax.broadcasted_iota(jnp.int32, (1, n_tail), 1)
        lam = _diff_lambda(lp_ref, lam_init)
        update(lambda h: tail_ref[:, h * LANES:(h + 1) * LANES],
               lambda h: tail_ref[:, width + h * LANES:width + (h + 1) * LANES], kpos)
        for h in range(DA_HEADS):
            hs = slice(h * 2 * tq, (h + 1) * 2 * tq)
            o_ref[:, h * LANES:(h + 1) * LANES] = _diff_finish(
                m_scr.at[hs], l_scr.at[hs], acc_scr.at[hs], lam, sg_ref[...], lam_init, tq)


def _diff_sample(projb_s, tail, cache4, e, page_table, lam_par, subln, lam_init, past_len):
    s, tq, _ = projb_s.shape
    n_pages = page_table.shape[1]
    page_size = cache4.shape[2] // (2 * DA_HEADS)
    pages_per_chunk = math.gcd(n_pages, 16)
    n_chunks = n_pages // pages_per_chunk
    width = DA_HEADS * DA_VDIM
    tk = math.gcd(pages_per_chunk * page_size, 512)
    kern = functools.partial(_diff_sample_kernel, e=e, n_chunks=n_chunks, pages_per_chunk=pages_per_chunk,
                             page_size=page_size, past_len=past_len, tk=tk, lam_init=lam_init)
    grid_spec = pltpu.PrefetchScalarGridSpec(
        num_scalar_prefetch=1,
        grid=(s, n_chunks + 1),
        in_specs=[
            pl.BlockSpec((None, tq, width), lambda si, ci, pt: (si, 0, 0)),
            pl.BlockSpec((None, tail.shape[1], 2 * width), lambda si, ci, pt: (si, 0, 0)),
            pl.BlockSpec((4, HEAD_DIM), lambda si, ci, pt: (0, 0)),
            pl.BlockSpec((1, DA_VDIM), lambda si, ci, pt: (0, 0)),
            pl.BlockSpec(memory_space=pl.ANY),
        ],
        out_specs=pl.BlockSpec((None, tq, width), lambda si, ci, pt: (si, 0, 0)),
        scratch_shapes=[
            pltpu.VMEM((pages_per_chunk * page_size * 2 * DA_HEADS, LANES), F32),
            pltpu.VMEM((pages_per_chunk * page_size * 2 * DA_HEADS, LANES), F32),
            pltpu.SemaphoreType.DMA((pages_per_chunk,)),
            pltpu.SemaphoreType.DMA((pages_per_chunk,)),
            pltpu.VMEM((DA_HEADS * 2 * tq, 1), F32),
            pltpu.VMEM((DA_HEADS * 2 * tq, 1), F32),
            pltpu.VMEM((DA_HEADS * 2 * tq, LANES), F32),
        ],
    )
    return pl.pallas_call(
        kern,
        grid_spec=grid_spec,
        out_shape=jax.ShapeDtypeStruct((s, tq, width), F32),
        compiler_params=_cparams(("arbitrary", "arbitrary")),
    )(page_table, projb_s, tail, lam_par, subln.reshape(1, DA_VDIM), cache4)


def _compress_compute(xk_ref, xv_ref, pe_ref, w1_ref, w2_ref, kg_ref, kcb_ref, vcb_ref, nb):
    def body(l, accs):
        pe = pe_ref[pl.ds(l, 1), :]
        ak = (xk_ref[pl.ds(l, nb, stride=NSA_BLOCK), :] + pe[:, 0:LANES]).astype(BF16)
        av = (xv_ref[pl.ds(l, nb, stride=NSA_BLOCK), :] + pe[:, LANES:2 * LANES]).astype(BF16)
        hk = jnp.dot(ak, w1_ref[0, l], preferred_element_type=F32)
        hv = jnp.dot(av, w1_ref[1, l], preferred_element_type=F32)
        return accs[0] + hk, accs[1] + hv

    zero = jnp.zeros((nb, 2 * NSA_CMP_HID), F32)
    hk, hv = lax.fori_loop(0, NSA_BLOCK, body, (zero, zero))
    ck = jnp.dot(jax.nn.gelu(hk).astype(BF16), w2_ref[0], preferred_element_type=F32)
    cv = jnp.dot(jax.nn.gelu(hv).astype(BF16), w2_ref[1], preferred_element_type=F32)
    ck = _seg_rms(ck, kg_ref[...])
    for ref in (kcb_ref, vcb_ref):
        ref[...] = jnp.zeros(ref.shape, ref.dtype)
    kcb_ref[0:nb, :] = ck.astype(BF16)
    vcb_ref[0:nb, :] = cv.astype(BF16)


def _compress_prompt_kernel(xk_ref, xv_ref, pe_ref, w1_ref, w2_ref, kg_ref, kcb_ref, vcb_ref, *, nb):
    _compress_compute(xk_ref, xv_ref, pe_ref, w1_ref, w2_ref, kg_ref, kcb_ref, vcb_ref, nb)


def _compress_specs(nbp):
    hid2 = 2 * NSA_CMP_HID
    w_specs = [
        pl.BlockSpec((NSA_BLOCK, 2 * LANES), lambda *a: (0, 0)),
        pl.BlockSpec((2, NSA_BLOCK, LANES, hid2), lambda *a: (0, 0, 0, 0)),
        pl.BlockSpec((2, hid2, LANES), lambda *a: (0, 0, 0)),
        pl.BlockSpec((1, LANES), lambda *a: (0, 0)),
    ]
    out_specs = [pl.BlockSpec((None, nbp, LANES), lambda bi, *a: (bi, 0, 0))] * 2
    return w_specs, out_specs


def _compress_out_shape(nbat, nbp):
    return [jax.ShapeDtypeStruct((nbat, nbp, LANES), BF16)] * 2


def _compress_prompt(projf, cw):
    b, t, _ = projf.shape
    nb = t // NSA_BLOCK
    nbp = -(-nb // LANES) * LANES
    w_specs, out_specs = _compress_specs(nbp)
    return pl.pallas_call(
        functools.partial(_compress_prompt_kernel, nb=nb),
        grid=(b,),
        in_specs=[pl.BlockSpec((None, t, LANES), lambda bi: (bi, 0, C_NKV // LANES)),
                  pl.BlockSpec((None, t, LANES), lambda bi: (bi, 0, C_NKV // LANES + 1))] + w_specs,
        out_specs=out_specs,
        out_shape=_compress_out_shape(b, nbp),
        compiler_params=_cparams(("parallel",)),
    )(projf, projf, *cw)


def _feature_page_copies(cache_ref, e, pt_ref, seq, n_pages, row0, n_rows, buf, sem):
    copies = []
    for j in range(n_pages):
        copies.append(pltpu.make_async_copy(
            cache_ref.at[e, pt_ref[seq, j], pl.ds(row0, n_rows), :],
            buf.at[pl.ds(j * n_rows, n_rows), :],
            sem.at[j]))
    return copies


def _per_sequence_pages(cache_ref, e, pt_ref, n_pages, row0, n_rows, bufs, sems, use):
    s_idx = pl.program_id(0)

    def run(slot):
        def copies(seq, which):
            return _feature_page_copies(cache_ref, e, pt_ref, seq, n_pages, row0, n_rows, bufs[which], sems[which])

        @pl.when(s_idx == 0)
        def _():
            for cp in copies(s_idx, slot):
                cp.start()

        @pl.when(s_idx + 1 < pl.num_programs(0))
        def _():
            for cp in copies(s_idx + 1, 1 - slot):
                cp.start()

        for cp in copies(s_idx, slot):
            cp.wait()
        use(bufs[slot])

    for slot in range(2):
        pl.when((s_idx & 1) == slot)(functools.partial(run, slot))


def _compress_sample_kernel(pt_ref, tail_ref, pe_ref, w1_ref, w2_ref, kg_ref, cache_ref,
                            kcb_ref, vcb_ref, buf0, buf1, sem0, sem1, *, e, n_pages, half_stride):
    feat = 2 * LANES
    compute = functools.partial(_compress_pages, tail_ref, pe_ref, w1_ref, w2_ref, kg_ref, kcb_ref, vcb_ref,
                                n_pages=n_pages, half_stride=half_stride)
    _per_sequence_pages(cache_ref, e, pt_ref, n_pages, 0, feat, (buf0, buf1), (sem0, sem1), compute)


def _compress_pages(tail_ref, pe_ref, w1_ref, w2_ref, kg_ref, kcb_ref, vcb_ref, buf, *, n_pages, half_stride):
    feat = 2 * LANES
    buf[n_pages * feat:, :] = tail_ref[...]
    n_pp = n_pages + 1

    def body(d, accs):
        out = []
        for c in range(2):
            for g in range(NSA_GROUPS):
                f = c * LANES + g * HEAD_DIM + d
                a = (buf[pl.ds(f, n_pp, stride=feat), :] + pe_ref[pl.ds(f, 1), :]).astype(BF16)
                out.append(accs[c * NSA_GROUPS + g] + jnp.dot(a, w1_ref[c, d], preferred_element_type=F32))
        return tuple(out)

    zero = jnp.zeros((n_pp, 2 * NSA_CMP_HID), F32)
    accs = lax.fori_loop(0, HEAD_DIM, body, (zero,) * (2 * NSA_GROUPS))
    pad = jnp.zeros((-n_pp % (2 * SUBLANES), LANES), F32)
    for c, out_ref in ((0, kcb_ref), (1, vcb_ref)):
        out_ref[...] = jnp.zeros(out_ref.shape, out_ref.dtype)
        for half in range(2):
            hs = slice(half * NSA_CMP_HID, (half + 1) * NSA_CMP_HID)
            hid = jnp.concatenate([accs[c * NSA_GROUPS + g][:, hs] for g in range(NSA_GROUPS)], axis=1)
            y = jnp.dot(jax.nn.gelu(hid).astype(BF16), w2_ref[c], preferred_element_type=F32)
            if c == 0:
                y = _seg_rms(y, kg_ref[...])
            y = jnp.concatenate([y, pad], axis=0).astype(BF16)
            out_ref[half * half_stride:half * half_stride + y.shape[0], :] = y


def _compress_sample(tail_t, cache_t, e, page_table, cw_t):
    s = tail_t.shape[0]
    n_pages = page_table.shape[1]
    page_size = cache_t.shape[3]
    assert page_size == 2 * NSA_BLOCK
    feat = 2 * LANES
    half_stride = -(-(n_pages + 1) // LANES) * LANES
    nbp = 2 * half_stride
    hid2 = 2 * NSA_CMP_HID
    grid_spec = pltpu.PrefetchScalarGridSpec(
        num_scalar_prefetch=1,
        grid=(s,),
        in_specs=[
            pl.BlockSpec((None, feat, page_size), lambda si, pt: (si, 0, 0)),
            pl.BlockSpec((feat, page_size), lambda si, pt: (0, 0)),
            pl.BlockSpec((2, HEAD_DIM, LANES, hid2), lambda si, pt: (0, 0, 0, 0)),
            pl.BlockSpec((2, hid2, LANES), lambda si, pt: (0, 0, 0)),
            pl.BlockSpec((1, LANES), lambda si, pt: (0, 0)),
            pl.BlockSpec(memory_space=pl.ANY),
        ],
        out_specs=[pl.BlockSpec((None, nbp, LANES), lambda si, pt: (si, 0, 0))] * 2,
        scratch_shapes=[pltpu.VMEM(((n_pages + 1) * feat, page_size), F32)] * 2
        + [pltpu.SemaphoreType.DMA((n_pages,))] * 2,
    )
    return pl.pallas_call(
        functools.partial(_compress_sample_kernel, e=e, n_pages=n_pages, half_stride=half_stride),
        grid_spec=grid_spec,
        out_shape=_compress_out_shape(s, nbp),
        compiler_params=_cparams(("arbitrary",)),
    )(page_table, tail_t, *cw_t, cache_t)


def _nsa_branches(q, kc_ref, vc_ref, k_sel_tile, vt_sel_tile, k_win_tile, vt_win_tile,
                  sel_scr, m_s, l_s, a_s, m_w, l_w, a_w,
                  *, tq, q0, n_blk, n_sel_rows, tk, wbase, n_win_rows, tkw,
                  row_block=lambda row: row, block_row=lambda blk: blk):
    nh = NSA_HEADS
    n_rows = nh * tq
    blk_shift = NSA_BLOCK.bit_length() - 1
    tq_shift = tq.bit_length() - 1
    assert 1 << tq_shift == tq and 1 << blk_shift == NSA_BLOCK
    lane128 = lax.broadcasted_iota(jnp.int32, (tq, LANES), 1)
    qf = q.astype(F32) * (HEAD_DIM ** -0.5 * LOG2E)
    parts = []
    for hh in range(nh):
        g = hh // NSA_REP
        blk = qf[:, (hh // 2) * LANES:(hh // 2 + 1) * LANES]
        if hh % 2 != g:
            blk = pltpu.roll(blk, HEAD_DIM, 1)
        in_seg = (lane128 >= g * HEAD_DIM) & (lane128 < (g + 1) * HEAD_DIM)
        parts.append(jnp.where(in_seg, blk, 0.0))
    qst = jnp.concatenate(parts, axis=0).T.astype(BF16)

    lane = lax.broadcasted_iota(jnp.int32, (1, n_rows), 1)
    qpos = q0 + (lane & (tq - 1))

    nbp = kc_ref.shape[0]
    nb8 = sel_scr.shape[0]
    assert n_blk <= nb8 <= nbp
    sc = jnp.dot(kc_ref[...], qst, preferred_element_type=F32)
    blk_p = row_block(lax.broadcasted_iota(jnp.int32, (nbp, 1), 0))
    cmp_ok = blk_p < ((qpos + 1) >> blk_shift)
    sc = jnp.where(cmp_ok, sc, NEG)
    mc = jnp.max(sc, axis=0, keepdims=True)
    ec = jnp.where(cmp_ok, jnp.exp2(sc - mc), 0.0)
    pc_all = ec / jnp.maximum(jnp.sum(ec, axis=0, keepdims=True), 1e-30)
    vct = vc_ref[...].astype(F32).T.astype(BF16)
    o_cmp = jnp.dot(vct, pc_all.astype(BF16), preferred_element_type=F32)
    pc = pc_all[0:nb8]
    blk_i = blk_p[0:nb8]
    blk_f = blk_i.astype(F32)

    n_pick = min(NSA_TOPK - 1, n_blk)

    def pick(imp, cur):
        cand = blk_i < cur
        x = jnp.where(cand, imp, -1.0)
        x = jnp.where(blk_i < n_blk, x, -2.0)
        picked = jnp.zeros(x.shape, F32)
        for _ in range(n_pick):
            mx = jnp.max(x, axis=0, keepdims=True)
            first = jnp.min(jnp.where(x == mx, blk_f, 3e38), axis=0, keepdims=True)
            hit = blk_f == first
            picked = jnp.where(hit, 1.0, picked)
            x = jnp.where(hit, -3.0, x)
        chosen = (blk_i == cur) | (cand & (picked > 0.5))
        return jnp.where(chosen, 0.0, NEG)

    if tq % LANES == 0:
        cur_t = (q0 + lax.broadcasted_iota(jnp.int32, (1, tq), 1)) >> blk_shift
        for g in range(NSA_GROUPS):
            base = g * NSA_REP * tq
            imp = pc[:, base:base + tq]
            for r in range(1, NSA_REP):
                imp = imp + pc[:, base + r * tq:base + (r + 1) * tq]
            sel_g = pick(imp, cur_t)
            for r in range(NSA_REP):
                sel_scr[:, base + r * tq:base + (r + 1) * tq] = sel_g
    else:
        assert n_rows == LANES
        rep = (lane >> tq_shift) & (NSA_REP - 1)
        imp = pc
        for d in range(1, NSA_REP):
            up = pltpu.roll(pc, LANES - d * tq, 1)
            dn = pltpu.roll(pc, d * tq, 1)
            imp = imp + jnp.where(rep + d < NSA_REP, up, 0.0) + jnp.where(rep >= d, dn, 0.0)
        sel_scr[...] = pick(imp, qpos >> blk_shift)

    _flash_init_t(m_s, l_s, a_s)
    blocks_per_tile = tk // NSA_BLOCK

    def sel_tile(ki, causal):
        k0 = pl.multiple_of(ki * tk, tk)
        st = jnp.dot(k_sel_tile(k0), qst, preferred_element_type=F32)
        b0 = ki * blocks_per_tile
        kpos = k0 + lax.broadcasted_iota(jnp.int32, (tk, 1), 0)
        pieces = []
        for bb in range(blocks_per_tile):
            rs = slice(bb * NSA_BLOCK, (bb + 1) * NSA_BLOCK)
            piece = st[rs] + sel_scr[pl.ds(block_row(b0 + bb), 1), :]
            pieces.append(jnp.where(kpos[rs] <= qpos, piece, NEG) if causal else piece)
        _flash_update_t(jnp.concatenate(pieces, axis=0), vt_sel_tile(k0), m_s, l_s, a_s)

    def sel_full(ki, carry):
        sel_tile(ki, False)
        return carry

    def sel_diag(ki, carry):
        sel_tile(ki, True)
        return carry

    n_sel = jnp.minimum((q0 + tq - 1) // tk, n_sel_rows // tk - 1) + 1
    n_full = jnp.minimum(q0 // tk, n_sel)
    lax.fori_loop(0, n_full, sel_full, 0)
    lax.fori_loop(n_full, n_sel, sel_diag, 0)

    _flash_init_t(m_w, l_w, a_w)

    def win_body(ki, carry):
        k0 = pl.multiple_of(ki * tkw, tkw)
        st = jnp.dot(k_win_tile(k0), qst, preferred_element_type=F32)
        kpos = wbase + k0 + lax.broadcasted_iota(jnp.int32, (tkw, 1), 0)
        kpos = jnp.where(kpos >= 0, kpos, jnp.iinfo(jnp.int32).max)
        dist = qpos - kpos
        ok = (dist >= 0) & (dist <= NSA_WINDOW)
        _flash_update_t(jnp.where(ok, st, NEG), vt_win_tile(k0), m_w, l_w, a_w)
        return carry

    lo_tile = jnp.maximum(q0 - NSA_WINDOW - wbase, 0) // tkw
    hi_tile = jnp.minimum((q0 + tq - 1 - wbase) // tkw, n_win_rows // tkw - 1)
    lax.fori_loop(lo_tile, hi_tile + 1, win_body, 0)

    o_sel = a_s[...] / jnp.maximum(l_s[...], 1e-30)
    o_win = a_w[...] / jnp.maximum(l_w[...], 1e-30)
    return o_cmp, o_sel, o_win


def _nsa_gate_lanes(o_cmp, o_sel, o_win, gate_logits, tq):
    gt = jax.nn.sigmoid(gate_logits).T
    def gate_row(branch):
        return jnp.concatenate([gt[3 * hh + branch:3 * hh + branch + 1, :] for hh in range(NSA_HEADS)], axis=1)
    ot = gate_row(0) * o_cmp + gate_row(1) * o_sel + gate_row(2) * o_win
    outs = []
    for j in range(NSA_HEADS // 2):
        g = (2 * j) // NSA_REP
        rs = slice(g * HEAD_DIM, (g + 1) * HEAD_DIM)
        pair = jnp.concatenate([ot[rs, (2 * j) * tq:(2 * j + 1) * tq],
                                ot[rs, (2 * j + 1) * tq:(2 * j + 2) * tq]], axis=0)
        outs.append(pair.T)
    return jnp.concatenate(outs, axis=1)


def _nsa_gate_rows(o_cmp, o_sel, o_win, gate_logits, tq):
    oc, os_, ow = o_cmp.T, o_sel.T, o_win.T
    gates = jax.nn.sigmoid(gate_logits)
    lo_half = lax.broadcasted_iota(jnp.int32, (tq, LANES), 1) < HEAD_DIM
    heads = []
    for hh in range(NSA_HEADS):
        g = hh // NSA_REP
        rs = slice(hh * tq, (hh + 1) * tq)
        o = (gates[:, 3 * hh:3 * hh + 1] * oc[rs] + gates[:, 3 * hh + 1:3 * hh + 2] * os_[rs]
             + gates[:, 3 * hh + 2:3 * hh + 3] * ow[rs])
        if hh % 2 != g:
            o = pltpu.roll(o, HEAD_DIM, 1)
        heads.append(o)
    return jnp.concatenate([jnp.where(lo_half, heads[2 * j], heads[2 * j + 1])
                            for j in range(NSA_HEADS // 2)], axis=1)


def _nsa_scratch_t(tq, nb8):
    n_rows = NSA_HEADS * tq
    one = [pltpu.VMEM((1, n_rows), F32), pltpu.VMEM((1, n_rows), F32), pltpu.VMEM((LANES, n_rows), F32)]
    return [pltpu.VMEM((nb8, n_rows), F32)] + one + one


def _nsa_prompt_kernel_t(q_ref, g_ref, kc_ref, vc_ref, ks_ref, vst_ref, kw_ref, vwt_ref, o_ref,
                         sel_scr, m_s, l_s, a_s, m_w, l_w, a_w, *, tq, tk, tkw, n_blk, t):
    q0 = pl.program_id(1) * tq
    branches = _nsa_branches(
        q_ref[...], kc_ref, vc_ref,
        lambda k0: ks_ref[pl.ds(k0, tk), :], lambda k0: vst_ref[:, pl.ds(k0, tk)],
        lambda k0: kw_ref[pl.ds(k0, tkw), :], lambda k0: vwt_ref[:, pl.ds(k0, tkw)],
        sel_scr, m_s, l_s, a_s, m_w, l_w, a_w,
        tq=tq, q0=q0, n_blk=n_blk, n_sel_rows=t, tk=tk, wbase=0, n_win_rows=t, tkw=tkw)
    o_ref[...] = _nsa_gate_lanes(*branches, g_ref[...], tq)


def _nsa_prompt_t(projf, projb, vst, vwt, kcb, vcb, tq, tk):
    b, t, _ = projb.shape
    nbp = kcb.shape[1]
    n_blk = t // NSA_BLOCK
    nb8 = -(-n_blk // (2 * SUBLANES)) * (2 * SUBLANES)
    tkw = min(tq, 2 * LANES)
    kern = functools.partial(_nsa_prompt_kernel_t, tq=tq, tk=tk, tkw=tkw, n_blk=n_blk, t=t)
    ks_blk = (C_NKV + 2 * LANES) // LANES
    kw_blk = (C_NKV + 4 * LANES) // LANES
    return pl.pallas_call(
        kern,
        grid=(b, t // tq),
        in_specs=[
            pl.BlockSpec((None, tq, 512), lambda bi, qi: (bi, qi, C_NQ // 512)),
            pl.BlockSpec((None, tq, LANES), lambda bi, qi: (bi, qi, C_NG // LANES)),
            pl.BlockSpec((None, nbp, LANES), lambda bi, qi: (bi, 0, 0)),
            pl.BlockSpec((None, nbp, LANES), lambda bi, qi: (bi, 0, 0)),
            pl.BlockSpec((None, t, LANES), lambda bi, qi: (bi, 0, ks_blk)),
            pl.BlockSpec((None, LANES, t), lambda bi, qi: (bi, 0, 0)),
            pl.BlockSpec((None, t, LANES), lambda bi, qi: (bi, 0, kw_blk)),
            pl.BlockSpec((None, LANES, t), lambda bi, qi: (bi, 0, 0)),
        ],
        out_specs=pl.BlockSpec((None, tq, 512), lambda bi, qi: (bi, qi, 0)),
        out_shape=jax.ShapeDtypeStruct((b, t, NSA_HEADS * HEAD_DIM), F32),
        scratch_shapes=_nsa_scratch_t(tq, nb8),
        compiler_params=_cparams(("parallel", "arbitrary")),
    )(projb, projf, kcb, vcb, projb, vst, projb, vwt)


def _nsa_sample_kernel_t(pt_ref, q_ref, g_ref, kc_ref, vc_ref, tail_ref, kwv_ref, cache_ref, o_ref,
                         buf0, buf1, sem0, sem1, sel_scr, m_s, l_s, a_s, m_w, l_w, a_w,
                         *, e, n_pages, page_size, past_len, n_blk, tkw, wbase, half_stride, pages_per_tile):
    feat = 2 * LANES
    compute = functools.partial(
        _nsa_sample_pages, q_ref, g_ref, kc_ref, vc_ref, tail_ref, kwv_ref, o_ref,
        sel_scr, m_s, l_s, a_s, m_w, l_w, a_w, n_pages=n_pages, page_size=page_size, past_len=past_len,
        n_blk=n_blk, tkw=tkw, wbase=wbase, half_stride=half_stride, pages_per_tile=pages_per_tile)
    _per_sequence_pages(cache_ref, e, pt_ref, n_pages, feat, feat, (buf0, buf1), (sem0, sem1), compute)


def _nsa_sample_pages(q_ref, g_ref, kc_ref, vc_ref, tail_ref, kwv_ref, o_ref,
                      sel_scr, m_s, l_s, a_s, m_w, l_w, a_w, buf,
                      *, n_pages, page_size, past_len, n_blk, tkw, wbase, half_stride, pages_per_tile):
    feat = 2 * LANES
    buf[n_pages * feat:, :] = tail_ref[...]
    tq = q_ref.shape[0]
    tk = pages_per_tile * page_size

    def page_rows(k0, j, first):
        return pl.ds(pl.multiple_of((k0 // page_size + j) * feat + first, LANES), LANES)

    def k_tile(k0):
        return jnp.concatenate([buf[page_rows(k0, j, 0), :].T for j in range(pages_per_tile)],
                               axis=0).astype(BF16)

    def vt_tile(k0):
        return jnp.concatenate([buf[page_rows(k0, j, LANES), :] for j in range(pages_per_tile)],
                               axis=1).astype(BF16)

    def row_block(row):
        half = (row >= half_stride).astype(jnp.int32)
        page = row - half * half_stride
        return jnp.where(page <= n_pages, 2 * page + half, jnp.iinfo(jnp.int32).max)

    branches = _nsa_branches(
        q_ref[...], kc_ref, vc_ref,
        k_tile, vt_tile,
        lambda k0: kwv_ref[pl.ds(k0, tkw), 0:LANES].astype(BF16),
        lambda k0: kwv_ref[pl.ds(k0, tkw), LANES:2 * LANES].T.astype(BF16),
        sel_scr, m_s, l_s, a_s, m_w, l_w, a_w,
        tq=tq, q0=past_len, n_blk=n_blk, n_sel_rows=(n_pages + 1) * page_size, tk=tk,
        wbase=wbase, n_win_rows=kwv_ref.shape[0], tkw=tkw,
        row_block=row_block, block_row=lambda blk: (blk & 1) * half_stride + (blk >> 1))
    o_ref[...] = _nsa_gate_rows(*branches, g_ref[...], tq)


def _nsa_sample_t(projf_s, projb_s, kcb, vcb, sel_tail_t, kwv, cache_t, e, page_table, past_len, n_valid):
    s, tq, _ = projb_s.shape
    n_pages = page_table.shape[1]
    page_size = cache_t.shape[3]
    feat = 2 * LANES
    nbp = kcb.shape[1]
    half_stride = nbp // 2
    n_blk = -(-(past_len + n_valid) // NSA_BLOCK)
    nb8 = nbp
    tkw = math.gcd(kwv.shape[1], LANES)
    wbase = past_len - (kwv.shape[1] - page_size)
    assert page_size == LANES
    pages_per_tile = max(p for p in range(1, 9) if (n_pages + 1) % p == 0)
    kern = functools.partial(_nsa_sample_kernel_t, e=e, n_pages=n_pages, page_size=page_size,
                             past_len=past_len, n_blk=n_blk, tkw=tkw, wbase=wbase, half_stride=half_stride,
                             pages_per_tile=pages_per_tile)
    grid_spec = pltpu.PrefetchScalarGridSpec(
        num_scalar_prefetch=1,
        grid=(s,),
        in_specs=[
            pl.BlockSpec((None, tq, 512), lambda si, pt: (si, 0, C_NQ // 512)),
            pl.BlockSpec((None, tq, LANES), lambda si, pt: (si, 0, C_NG // LANES)),
            pl.BlockSpec((None, nbp, LANES), lambda si, pt: (si, 0, 0)),
            pl.BlockSpec((None, nbp, LANES), lambda si, pt: (si, 0, 0)),
            pl.BlockSpec((None, feat, page_size), lambda si, pt: (si, 0, 0)),
            pl.BlockSpec((None, kwv.shape[1], 2 * LANES), lambda si, pt: (si, 0, 0)),
            pl.BlockSpec(memory_space=pl.ANY),
        ],
        out_specs=pl.BlockSpec((None, tq, 512), lambda si, pt: (si, 0, 0)),
        scratch_shapes=[pltpu.VMEM(((n_pages + 1) * feat, page_size), F32)] * 2
        + [pltpu.SemaphoreType.DMA((n_pages,))] * 2 + _nsa_scratch_t(tq, nb8),
    )
    return pl.pallas_call(
        kern,
        grid_spec=grid_spec,
        out_shape=jax.ShapeDtypeStruct((s, tq, NSA_HEADS * HEAD_DIM), F32),
        compiler_params=_cparams(("arbitrary",)),
    )(page_table, projb_s, projf_s, kcb, vcb, sel_tail_t, kwv, cache_t)


def _split3(x):
    hi = x.astype(BF16)
    r1 = x - hi.astype(F32)
    mid = r1.astype(BF16)
    lo = (r1 - mid.astype(F32)).astype(BF16)
    return hi, mid, lo


def _hgrn_kernel(q_ref, f_ref, i_ref, g_ref, lb_ref, ng_ref, s0_ref, o_ref, sfin_ref,
                 st_scr, cum_scr, k_scr, v_scr, *, chunk, n_chunks, n_valid):
    ti = pl.program_id(2)

    @pl.when(ti == 0)
    def _():
        st_scr[...] = s0_ref[...]

    lb = lb_ref[...]
    tri = (lax.broadcasted_iota(jnp.int32, (chunk, chunk), 0)
           >= lax.broadcasted_iota(jnp.int32, (chunk, chunk), 1)).astype(BF16)
    row = lax.broadcasted_iota(jnp.int32, (chunk, 1), 0)
    row8 = lax.broadcasted_iota(jnp.int32, (SUBLANES, 1), 0)
    n_sub = chunk // SUBLANES

    def chunk_body(c, carry):
        r0 = pl.multiple_of(c * chunk, chunk)
        qr = q_ref[pl.ds(r0, chunk), :]
        q = qr * jax.nn.sigmoid(qr) * (HG_DK ** -0.5)
        fg = lb + (1.0 - lb) * jax.nn.sigmoid(f_ref[pl.ds(r0, chunk), :])
        logf = jnp.log(fg)
        k = 1.0 - fg
        v = i_ref[pl.ds(r0, chunk), :]
        if n_valid is not None:
            live = (ti * (chunk * n_chunks) + r0 + row) < n_valid
            logf = jnp.where(live, logf, 0.0)
            k = jnp.where(live, k, 0.0)
        cum = sum(jnp.dot(tri, part, preferred_element_type=F32) for part in _split3(logf))
        cum2 = cum * LOG2E
        cum_scr[...] = cum2
        k_scr[...] = k
        v_scr[...] = v
        st = st_scr[...]
        inter = _nt_dot((q * jnp.exp2(cum2)).astype(BF16), st.astype(BF16))

        q_sub = [q[i * SUBLANES:(i + 1) * SUBLANES] for i in range(n_sub)]
        c_sub = [cum2[i * SUBLANES:(i + 1) * SUBLANES] for i in range(n_sub)]
        prods = []
        for s in range(chunk):
            j = s // SUBLANES
            cs = jnp.broadcast_to(cum_scr[s:s + 1, :], (SUBLANES, HG_DK))
            ks = jnp.broadcast_to(k_scr[s:s + 1, :], (SUBLANES, HG_DK))
            for i in range(j, n_sub):
                dec = jnp.exp2(c_sub[i] - cs)
                if i == j:
                    dec = jnp.where(row8 >= s - j * SUBLANES, dec, 0.0)
                prods.append(q_sub[i] * ks * dec)
        sums = jnp.dot(jnp.concatenate(prods, axis=0).astype(BF16), jnp.ones((HG_DK, HG_DK), BF16),
                       preferred_element_type=F32)
        o_sub = [inter[i * SUBLANES:(i + 1) * SUBLANES] for i in range(n_sub)]
        n = 0
        for s in range(chunk):
            vs = jnp.broadcast_to(v_scr[s:s + 1, :], (SUBLANES, HG_DK))
            for i in range(s // SUBLANES, n_sub):
                o_sub[i] = o_sub[i] + sums[n * SUBLANES:(n + 1) * SUBLANES] * vs
                n += 1
        o = jnp.concatenate(o_sub, axis=0)

        last = cum2[chunk - 1:chunk, :]
        kd = k * jnp.exp2(last - cum2)
        st_scr[...] = st * jnp.exp2(last) + jnp.dot(v.T.astype(BF16), kd.astype(BF16),
                                                     preferred_element_type=F32)
        gr = g_ref[pl.ds(r0, chunk), :]
        o_ref[pl.ds(r0, chunk), :] = _rms(o, ng_ref[...]) * (gr * jax.nn.sigmoid(gr))
        return carry

    lax.fori_loop(0, n_chunks, chunk_body, 0, unroll=2 if n_chunks % 2 == 0 else 1)

    @pl.when(ti == pl.num_programs(2) - 1)
    def _():
        sfin_ref[...] = st_scr[...]


def _hgrn(projf, lb, norm_gain, s0t, chunk, tt, n_valid):
    b, t, _ = projf.shape
    dk = HG_DK
    kern = functools.partial(_hgrn_kernel, chunk=chunk, n_chunks=tt // chunk, n_valid=n_valid)
    col = lambda j: pl.BlockSpec((None, tt, dk), lambda bi, h, ti: (bi, ti, j * HG_HEADS + h))
    vec = pl.BlockSpec((1, dk), lambda bi, h, ti: (0, h))
    st_spec = pl.BlockSpec((None, None, dk, dk), lambda bi, h, ti: (bi, h, 0, 0))
    return pl.pallas_call(
        kern,
        grid=(b, HG_HEADS, t // tt),
        in_specs=[col(0), col(1), col(2), col(3), vec, vec, st_spec],
        out_specs=[pl.BlockSpec((None, tt, dk), lambda bi, h, ti: (bi, ti, h)), st_spec],
        out_shape=[jax.ShapeDtypeStruct((b, t, HG_HEADS * dk), F32),
                   jax.ShapeDtypeStruct((b, HG_HEADS, dk, dk), F32)],
        scratch_shapes=[pltpu.VMEM((dk, dk), F32)] + [pltpu.VMEM((chunk, dk), F32)] * 3,
        compiler_params=_cparams(("parallel", "parallel", "arbitrary")),
    )(projf, projf, projf, projf, lb.reshape(1, -1), norm_gain.reshape(1, -1), s0t)


def _row_tile(n, pref):
    return math.gcd(n, pref)


def _pad_rows(x, rows):
    return jnp.pad(x, ((0, 0), (0, rows - x.shape[1]), (0, 0)))


def _even_weights(w_in, qk_a, qk_b, pe, w1, w2):
    w_pad = jnp.pad(w_in, ((0, 0), (0, EVEN_IN_PAD - EVEN_IN))).astype(BF16)
    ones = jnp.ones((HEAD_DIM,), F32)
    segs = [qk_a[0]] * 8 + [qk_a[1]] * 8 + [ones] * 8 + [qk_b[0]] * 8 + [ones] * 4 + [qk_b[2]] * 2 \
        + [ones] * 2 + [qk_b[3]] * 2 + [ones] * 4
    colgain = jnp.concatenate(segs).reshape(1, EVEN_IN_PAD).astype(F32)
    pe_cat = jnp.concatenate([pe[0], pe[0], pe[1], pe[1]], axis=-1).astype(F32)
    z1 = jnp.zeros_like(w1)
    w1bd = jnp.concatenate([jnp.concatenate([w1, z1], axis=-1), jnp.concatenate([z1, w1], axis=-1)],
                           axis=-2).astype(BF16)
    z2 = jnp.zeros_like(w2)
    w2bd = jnp.concatenate([jnp.concatenate([w2, z2], axis=-1), jnp.concatenate([z2, w2], axis=-1)],
                           axis=-2).astype(BF16)
    kgain = jnp.concatenate([qk_b[1], qk_b[1]]).reshape(1, LANES).astype(F32)
    pe_dl = jnp.swapaxes(pe, 1, 2)
    pe_dl = jnp.concatenate([pe_dl, pe_dl], axis=-1)
    pe_t = jnp.stack([pe_dl] * NSA_GROUPS, axis=1).reshape(2 * LANES, LANES).astype(F32)
    w1_dl = jnp.swapaxes(w1, 1, 2)
    z1t = jnp.zeros_like(w1_dl)
    w1t = jnp.concatenate([jnp.concatenate([w1_dl, z1t], axis=-1), jnp.concatenate([z1t, w1_dl], axis=-1)],
                          axis=-2).astype(BF16)
    return w_pad, colgain, (pe_cat, w1bd, w2bd, kgain), (pe_t, w1t, w2bd, kgain)


def kernel(x_prompt, x_sample, cache_diff_kv, cache_nsa_kv, cache_nsa_win, state_hgrn, page_table,
           norm_mix, norm_ffn, w_ffn_in, w_ffn_out, w_in_even, w_out_even, diff_qk_gain, diff_lambda,
           diff_subln_gain, nsa_qk_gain, nsa_cmp_pe, nsa_cmp_w1, nsa_cmp_w2, w_in_odd, w_out_odd,
           hgrn_norm_gain, hgrn_lb_logits):
    b, t, d = x_prompt.shape
    s, ts, _ = x_sample.shape
    depth = norm_mix.shape[0]
    n_even = cache_diff_kv.shape[0]
    n_pool, page_size = cache_diff_kv.shape[1], cache_diff_kv.shape[2]
    past_len = page_table.shape[1] * page_size
    n_buf = cache_nsa_win.shape[2]
    tsp = SAMPLE_ROWS
    assert ts <= tsp and d == D_MODEL

    lbw = jax.nn.softmax(hgrn_lb_logits.astype(F32), axis=0)
    lower_bounds = jnp.cumsum(lbw, axis=0) - lbw[0]

    hp = x_prompt.reshape(b * t, d)
    hs = _pad_rows(x_sample, tsp).reshape(s * tsp, d)
    tm_p = _row_tile(b * t, 512)
    tm_s = _row_tile(s * tsp, 256)
    cache_diff4 = cache_diff_kv.reshape(n_even, n_pool, page_size * 2 * DA_HEADS, DA_VDIM)
    cache_nsa_t = jnp.transpose(cache_nsa_kv, (0, 1, 3, 4, 5, 2)).reshape(
        n_even, n_pool, 4 * NSA_GROUPS * HEAD_DIM, page_size)
    cache_win = cache_nsa_win.reshape(n_even, s, n_buf, 2 * NSA_GROUPS * HEAD_DIM)
    live = (jnp.arange(tsp) < ts)[None, :, None]

    dkv_p, dkv_s, nkv_p, nkv_s, win_p, win_s, hg_p, hg_s = [], [], [], [], [], [], [], []
    for l in range(depth):
        if l % 2 == 0:
            e = l // 2
            lam_init = 0.8 - 0.6 * math.exp(-0.3 * l)
            w_pad, colgain, cw, cw_t = _even_weights(w_in_even[e], diff_qk_gain[e], nsa_qk_gain[e],
                                               nsa_cmp_pe[e], nsa_cmp_w1[e], nsa_cmp_w2[e])
            w_out = w_out_even[e].astype(BF16)
            pf, pb = _norm_proj(hp, norm_mix[l], w_pad, colgain, EVEN_NORM_BLOCKS, tm_p, True)
            pf3, pb3 = pf.reshape(b, t, -1), pb.reshape(b, t, -1)
            feat_major = lambda c0, c1: jnp.swapaxes(pb3[:, :, c0:c1], 1, 2)
            da = _diff_prompt(pb3, feat_major(C_DV, C_NQ), diff_lambda[e], diff_subln_gain[e], lam_init,
                              tq=_row_tile(t, 512), tk=_row_tile(t, 512))
            kcb, vcb = _compress_prompt(pf3, cw)
            onsa = _nsa_prompt_t(pf3, pb3, feat_major(C_NKV + 3 * LANES, C_NKV + 4 * LANES),
                                 feat_major(C_NKV + 5 * LANES, C_NKV + 6 * LANES), kcb, vcb,
                                 tq=_row_tile(t, 256), tk=_row_tile(t, 256))
            hp = _out_proj(da.reshape(b * t, -1), 0, onsa.reshape(b * t, -1), 0, hp, w_out, tm_p)
            dkv_p.append(pf3[:, :, C_DK:C_NQ].reshape(b, t, 2, DA_HEADS, DA_VDIM))
            nkv_p.append(pf3[:, :, C_NKV:C_NKV + 512].reshape(b, t, 4, NSA_GROUPS, HEAD_DIM))
            nw = min(NSA_WINDOW, t)
            win_p.append(pf3[:, t - nw:, C_NKV + 512:C_NG].reshape(b, nw, 2, NSA_GROUPS, HEAD_DIM))
            sf, sb = _norm_proj(hs, norm_mix[l], w_pad, colgain, EVEN_NORM_BLOCKS, tm_s, True)
            sf3, sb3 = sf.reshape(s, tsp, -1), sb.reshape(s, tsp, -1)
            diff_tail = _pad_rows(sf3[:, :, C_DK:C_NQ], LANES)
            da_s = _diff_sample(sb3, diff_tail, cache_diff4, e, page_table, diff_lambda[e],
                                diff_subln_gain[e], lam_init, past_len)
            as_page = lambda x: jnp.swapaxes(_pad_rows(x, page_size), 1, 2)
            cmp_tail = as_page(jnp.where(live, sf3[:, :, C_NKV:C_NKV + 256], 0.0))
            kcb_s, vcb_s = _compress_sample(cmp_tail, cache_nsa_t, e, page_table, cw_t)
            sel_tail = as_page(sf3[:, :, C_NKV + 256:C_NKV + 512])
            new_win = sf3[:, :, C_NKV + 512:C_NG]
            kwv = jnp.concatenate([cache_win[e], _pad_rows(new_win, page_size)], axis=1)
            onsa_s = _nsa_sample_t(sf3, sb3, kcb_s, vcb_s, sel_tail, kwv, cache_nsa_t, e, page_table,
                                   past_len, ts)
            hs = _out_proj(da_s.reshape(s * tsp, -1), 0, onsa_s.reshape(s * tsp, -1), 0, hs, w_out, tm_s)
            dkv_s.append(sf3[:, :ts, C_DK:C_NQ].reshape(s, ts, 2, DA_HEADS, DA_VDIM))
            nkv_s.append(sf3[:, :ts, C_NKV:C_NKV + 512].reshape(s, ts, 4, NSA_GROUPS, HEAD_DIM))
            win_all = jnp.concatenate([cache_win[e], new_win[:, :ts]], axis=1)[:, -n_buf:]
            win_s.append(win_all.reshape(s, n_buf, 2, NSA_GROUPS, HEAD_DIM))
        else:
            r = l // 2
            w_in = w_in_odd[r].astype(BF16)
            w_out = w_out_odd[r].astype(BF16)
            ones = jnp.ones((1, ODD_IN), F32)
            pf, = _norm_proj(hp, norm_mix[l], w_in, ones, (), tm_p, False)
            chunk = math.gcd(t, HG_CHUNK)
            o_p, st_p = _hgrn(pf.reshape(b, t, -1), lower_bounds[l], hgrn_norm_gain[r],
                              jnp.zeros((b, HG_HEADS, HG_DK, HG_DK), F32), chunk, _row_tile(t, 512), None)
            hp = _out_proj(o_p.reshape(b * t, -1), 0, o_p.reshape(b * t, -1), 1, hp, w_out, tm_p)
            hg_p.append(jnp.swapaxes(st_p, -1, -2))
            sf, = _norm_proj(hs, norm_mix[l], w_in, ones, (), tm_s, False)
            o_s, st_s = _hgrn(sf.reshape(s, tsp, -1), lower_bounds[l], hgrn_norm_gain[r],
                              jnp.swapaxes(state_hgrn[r].astype(F32), -1, -2), tsp, tsp, ts)
            hs = _out_proj(o_s.reshape(s * tsp, -1), 0, o_s.reshape(s * tsp, -1), 1, hs, w_out, tm_s)
            hg_s.append(jnp.swapaxes(st_s, -1, -2))
        w_fi = w_ffn_in[l].astype(BF16)
        w_fo = w_ffn_out[l].astype(BF16)
        tf = w_fo.shape[0] // 2
        hp = _ffn(hp, norm_ffn[l], w_fi, w_fo, tm_p, tf)
        hs = _ffn(hs, norm_ffn[l], w_fi, w_fo, tm_s, tf)

    y_s = hs.reshape(s, tsp, d)[:, :ts]
    return (hp.reshape(b, t, d), y_s, jnp.stack(dkv_p), jnp.stack(dkv_s), jnp.stack(nkv_p), jnp.stack(nkv_s),
            jnp.stack(win_p), jnp.stack(win_s), jnp.stack(hg_p), jnp.stack(hg_s))
```

```python
import functools
import math

import jax
import jax.numpy as jnp
from jax import lax
from jax.experimental import pallas as pl
from jax.experimental.pallas import tpu as pltpu

F32 = jnp.float32
BF16 = jnp.bfloat16

D_MODEL = 1024
HEAD_DIM = 64
DA_HEADS = 4
DA_VDIM = 2 * HEAD_DIM
NSA_HEADS = 8
NSA_GROUPS = 2
NSA_REP = NSA_HEADS // NSA_GROUPS
NSA_BLOCK = 64
NSA_TOPK = 16
NSA_WINDOW = 512
NSA_CMP_HID = 2 * HEAD_DIM
HG_HEADS = 8
HG_DK = D_MODEL // HG_HEADS
HG_CHUNK = 64
EPS = 1e-6
NEG = -1e30
M_FLOOR = -1e29
LOG2E = 1.4426950408889634

LANES = 128
SUBLANES = 8
VMEM_LIMIT = 56 * 1024 * 1024

C_DQ, C_DK, C_DV, C_NQ, C_NKV, C_NG = 0, 512, 1024, 1536, 2048, 2816
EVEN_IN = 2840
EVEN_IN_PAD = 2944
EVEN_NORM_BLOCKS = tuple(range(0, 8)) + tuple(range(12, 16)) + (18, 20)
ODD_IN = 4096
SAMPLE_ROWS = 16


def _cparams(sem):
    return pltpu.CompilerParams(dimension_semantics=sem, vmem_limit_bytes=VMEM_LIMIT)


def _nt_dot(a, b):
    return lax.dot_general(a, b, (((1,), (1,)), ((), ())), preferred_element_type=F32)


def _rms(x, gain):
    return x * lax.rsqrt(jnp.mean(x * x, axis=-1, keepdims=True) + EPS) * gain


def _seg_rms(y, gain):
    lo = lax.broadcasted_iota(jnp.int32, y.shape, 1) < HEAD_DIM
    y2 = y * y
    s_lo = jnp.sum(jnp.where(lo, y2, 0.0), axis=-1, keepdims=True)
    s_hi = jnp.sum(jnp.where(lo, 0.0, y2), axis=-1, keepdims=True)
    ms = jnp.where(lo, s_lo, s_hi) * (1.0 / HEAD_DIM)
    return y * lax.rsqrt(ms + EPS) * gain


def _norm_proj_kernel(x_ref, g_ref, w_ref, cg_ref, o_ref, *maybe_ob_ref, norm_blocks, col_chunk):
    xn = _rms(x_ref[...], g_ref[...]).astype(BF16)
    n_cols = w_ref.shape[1]
    for c0 in range(0, n_cols, col_chunk):
        c1 = min(c0 + col_chunk, n_cols)
        y = jnp.dot(xn, w_ref[:, c0:c1], preferred_element_type=F32)
        for b in range(c0 // LANES, c1 // LANES):
            yb = y[:, b * LANES - c0:(b + 1) * LANES - c0]
            if b in norm_blocks:
                yb = _seg_rms(yb, cg_ref[:, b * LANES:(b + 1) * LANES])
            o_ref[:, b * LANES:(b + 1) * LANES] = yb
            for ob_ref in maybe_ob_ref:
                ob_ref[:, b * LANES:(b + 1) * LANES] = yb.astype(BF16)


def _norm_proj(x2d, gain, w_bf16, colgain, norm_blocks, tm, with_bf16):
    n, d = x2d.shape
    c = w_bf16.shape[1]
    kern = functools.partial(_norm_proj_kernel, norm_blocks=norm_blocks, col_chunk=512)
    n_out = 2 if with_bf16 else 1
    return pl.pallas_call(
        kern,
        grid=(n // tm,),
        in_specs=[
            pl.BlockSpec((tm, d), lambda i: (i, 0)),
            pl.BlockSpec((1, d), lambda i: (0, 0)),
            pl.BlockSpec((d, c), lambda i: (0, 0)),
            pl.BlockSpec((1, c), lambda i: (0, 0)),
        ],
        out_specs=[pl.BlockSpec((tm, c), lambda i: (i, 0))] * n_out,
        out_shape=[jax.ShapeDtypeStruct((n, c), F32), jax.ShapeDtypeStruct((n, c), BF16)][:n_out],
        compiler_params=_cparams(("parallel",)),
    )(x2d, gain.reshape(1, d), w_bf16, colgain)


def _ffn_kernel(x_ref, g_ref, wg_ref, wu_ref, wo_ref, o_ref, xn_scr, acc_scr):
    f = pl.program_id(1)

    @pl.when(f == 0)
    def _():
        xn_scr[...] = _rms(x_ref[...], g_ref[...]).astype(BF16)
        acc_scr[...] = jnp.zeros_like(acc_scr)

    xn = xn_scr[...]
    gate = jnp.dot(xn, wg_ref[...], preferred_element_type=F32)
    up = jnp.dot(xn, wu_ref[...], preferred_element_type=F32)
    act = (gate * jax.nn.sigmoid(gate) * up).astype(BF16)
    acc_scr[...] += jnp.dot(act, wo_ref[...], preferred_element_type=F32)

    @pl.when(f == pl.num_programs(1) - 1)
    def _():
        o_ref[...] = x_ref[...] + acc_scr[...]


def _ffn(x2d, gain, w_in_bf16, w_out_bf16, tm, tf):
    n, d = x2d.shape
    dff = w_out_bf16.shape[0]
    nf = dff // tf
    return pl.pallas_call(
        _ffn_kernel,
        grid=(n // tm, nf),
        in_specs=[
            pl.BlockSpec((tm, d), lambda i, f: (i, 0)),
            pl.BlockSpec((1, d), lambda i, f: (0, 0)),
            pl.BlockSpec((d, tf), lambda i, f: (0, f)),
            pl.BlockSpec((d, tf), lambda i, f: (0, nf + f)),
            pl.BlockSpec((tf, d), lambda i, f: (f, 0)),
        ],
        out_specs=pl.BlockSpec((tm, d), lambda i, f: (i, 0)),
        out_shape=jax.ShapeDtypeStruct((n, d), F32),
        scratch_shapes=[pltpu.VMEM((tm, d), BF16), pltpu.VMEM((tm, d), F32)],
        compiler_params=_cparams(("parallel", "arbitrary")),
    )(x2d, gain.reshape(1, d), w_in_bf16, w_in_bf16, w_out_bf16)


def _out_proj_kernel(a_ref, b_ref, r_ref, w_ref, o_ref):
    half = a_ref.shape[1]
    y = jnp.dot(a_ref[...].astype(BF16), w_ref[0:half, :], preferred_element_type=F32)
    y += jnp.dot(b_ref[...].astype(BF16), w_ref[half:2 * half, :], preferred_element_type=F32)
    o_ref[...] = r_ref[...] + y


def _out_proj(a, a_blk, b, b_blk, resid, w_bf16, tm):
    n, d = resid.shape
    half = w_bf16.shape[0] // 2
    return pl.pallas_call(
        _out_proj_kernel,
        grid=(n // tm,),
        in_specs=[
            pl.BlockSpec((tm, half), lambda i: (i, a_blk)),
            pl.BlockSpec((tm, half), lambda i: (i, b_blk)),
            pl.BlockSpec((tm, d), lambda i: (i, 0)),
            pl.BlockSpec((2 * half, d), lambda i: (0, 0)),
        ],
        out_specs=pl.BlockSpec((tm, d), lambda i: (i, 0)),
        out_shape=jax.ShapeDtypeStruct((n, d), F32),
        compiler_params=_cparams(("parallel",)),
    )(a, b, resid, w_bf16)


def _flash_init(m_scr, l_scr, acc_scr):
    m_scr[...] = jnp.full(m_scr.shape, NEG, F32)
    l_scr[...] = jnp.zeros(l_scr.shape, F32)
    acc_scr[...] = jnp.zeros(acc_scr.shape, F32)


def _flash_update(s, mask, v_tile, m_scr, l_scr, acc_scr):
    s = jnp.where(mask, s, NEG)
    m_old = m_scr[...]
    m_new = jnp.maximum(m_old, jnp.max(s, axis=-1, keepdims=True))
    alpha = jnp.exp(m_old - m_new)
    p = jnp.where(mask, jnp.exp(s - m_new), 0.0)
    l_scr[...] = alpha * l_scr[...] + jnp.sum(p, axis=-1, keepdims=True)
    acc_scr[...] = alpha * acc_scr[...] + jnp.dot(p.astype(BF16), v_tile, preferred_element_type=F32)
    m_scr[...] = m_new


def _diff_lambda(lp_ref, lam_init):
    lp = lp_ref[...].astype(F32)
    a = jnp.sum(lp[0:1, :] * lp[1:2, :], axis=-1, keepdims=True)
    b = jnp.sum(lp[2:3, :] * lp[3:4, :], axis=-1, keepdims=True)
    return jnp.exp(a) - jnp.exp(b) + lam_init


def _flash_init_t(m_scr, l_scr, acc_scr):
    m_scr[...] = jnp.full(m_scr.shape, M_FLOOR, F32)
    l_scr[...] = jnp.zeros(l_scr.shape, F32)
    acc_scr[...] = jnp.zeros(acc_scr.shape, F32)


def _flash_update_t(st, vt_tile, m_scr, l_scr, acc_scr):
    m_old = m_scr[...]
    m_new = jnp.maximum(m_old, jnp.max(st, axis=0, keepdims=True))
    alpha = jnp.exp2(m_old - m_new)
    p = jnp.exp2(st - m_new)
    l_scr[...] = alpha * l_scr[...] + jnp.sum(p, axis=0, keepdims=True)
    acc_scr[...] = alpha * acc_scr[...] + jnp.dot(vt_tile, p.astype(BF16), preferred_element_type=F32)
    m_scr[...] = m_new


def _diff_qstack(q, scale):
    lo = lax.broadcasted_iota(jnp.int32, q.shape, 1) < HEAD_DIM
    qs = q.astype(F32) * scale
    return jnp.concatenate([jnp.where(lo, qs, 0.0), jnp.where(lo, 0.0, qs)], axis=0)


def _diff_finish(m_scr, l_scr, acc_scr, lam, gain, lam_init, tq):
    o = acc_scr[...] / jnp.maximum(l_scr[...], 1e-30)
    d = o[0:tq] - lam * o[tq:2 * tq]
    return _rms(d, gain) * (1.0 - lam_init)


def _diff_prompt_kernel(q_ref, k_ref, vt_ref, lp_ref, sg_ref, o_ref, m_scr, l_scr, acc_scr,
                        *, tq, tk, lam_init):
    assert tq & (tq - 1) == 0
    q0 = pl.program_id(2) * tq
    qst = _diff_qstack(q_ref[...], HEAD_DIM ** -0.5 * LOG2E).T.astype(BF16)
    lane = lax.broadcasted_iota(jnp.int32, (1, 2 * tq), 1)
    qpos = q0 + (lane & (tq - 1))
    _flash_init_t(m_scr, l_scr, acc_scr)

    def tile(ki, causal):
        k0 = pl.multiple_of(ki * tk, tk)
        st = jnp.dot(k_ref[pl.ds(k0, tk), :], qst, preferred_element_type=F32)
        if causal:
            kpos = k0 + lax.broadcasted_iota(jnp.int32, (tk, 1), 0)
            st = jnp.where(kpos <= qpos, st, NEG)
        _flash_update_t(st, vt_ref[:, pl.ds(k0, tk)], m_scr, l_scr, acc_scr)

    def full_tile(ki, carry):
        tile(ki, False)
        return carry

    def diag_tile(ki, carry):
        tile(ki, True)
        return carry

    n_full = q0 // tk
    lax.fori_loop(0, n_full, full_tile, 0)
    lax.fori_loop(n_full, (q0 + tq - 1) // tk + 1, diag_tile, 0)
    lam = _diff_lambda(lp_ref, lam_init)
    ot = acc_scr[...] / jnp.maximum(l_scr[...], 1e-30)
    dt = ot[:, 0:tq] - lam * ot[:, tq:2 * tq]
    dn = dt * lax.rsqrt(jnp.mean(dt * dt, axis=0, keepdims=True) + EPS)
    o_ref[...] = dn.T * sg_ref[...] * (1.0 - lam_init)


def _diff_prompt(projb, vt, lam_par, subln, lam_init, tq, tk):
    b, t, _ = projb.shape
    kern = functools.partial(_diff_prompt_kernel, tq=tq, tk=tk, lam_init=lam_init)
    kblk = C_DK // LANES
    return pl.pallas_call(
        kern,
        grid=(b, DA_HEADS, t // tq),
        in_specs=[
            pl.BlockSpec((None, tq, LANES), lambda bi, h, qi: (bi, qi, h)),
            pl.BlockSpec((None, t, LANES), lambda bi, h, qi: (bi, 0, kblk + h)),
            pl.BlockSpec((None, DA_VDIM, t), lambda bi, h, qi: (bi, h, 0)),
            pl.BlockSpec((4, HEAD_DIM), lambda bi, h, qi: (0, 0)),
            pl.BlockSpec((1, DA_VDIM), lambda bi, h, qi: (0, 0)),
        ],
        out_specs=pl.BlockSpec((None, tq, LANES), lambda bi, h, qi: (bi, qi, h)),
        out_shape=jax.ShapeDtypeStruct((b, t, DA_HEADS * DA_VDIM), F32),
        scratch_shapes=[pltpu.VMEM((1, 2 * tq), F32), pltpu.VMEM((1, 2 * tq), F32),
                        pltpu.VMEM((DA_VDIM, 2 * tq), F32)],
        compiler_params=_cparams(("parallel", "parallel", "arbitrary")),
    )(projb, projb, vt, lam_par, subln.reshape(1, DA_VDIM))


def _page_copies(cache_ref, e, pt_ref, seq, page0, n_pages, lane0, n_lanes, buf, sem, page_size):
    copies = []
    for j in range(n_pages):
        page = pt_ref[seq, page0 + j]
        copies.append(pltpu.make_async_copy(
            cache_ref.at[e, page, :, pl.ds(lane0, n_lanes)],
            buf.at[pl.ds(j * page_size, page_size), :],
            sem.at[j]))
    return copies


def _diff_sample_kernel(pt_ref, q_ref, tail_ref, lp_ref, sg_ref, cache_ref, o_ref,
                        buf0, buf1, sem0, sem1, m_scr, l_scr, acc_scr,
                        *, e, n_chunks, pages_per_chunk, page_size, past_len, tk, lam_init):
    s_idx = pl.program_id(0)
    c_idx = pl.program_id(1)
    bufs, sems = (buf0, buf1), (sem0, sem1)
    tq = SAMPLE_ROWS
    width = DA_HEADS * DA_VDIM
    assert tq & (tq - 1) == 0
    n_rows = DA_HEADS * 2 * tq
    row = lax.broadcasted_iota(jnp.int32, (n_rows, 1), 0)
    qpos = past_len + (row & (tq - 1))

    @pl.when(c_idx == 0)
    def _():
        _flash_init(m_scr, l_scr, acc_scr)

    qs = [_diff_qstack(q_ref[:, h * LANES:(h + 1) * LANES], HEAD_DIM ** -0.5).astype(BF16)
          for h in range(DA_HEADS)]

    def update(k_tile, v_tile, kpos):
        s = jnp.concatenate([_nt_dot(qs[h], k_tile(h).astype(BF16)) for h in range(DA_HEADS)], axis=0)
        mask = kpos <= qpos
        s = jnp.where(mask, s, NEG)
        m_old = m_scr[...]
        m_new = jnp.maximum(m_old, jnp.max(s, axis=-1, keepdims=True))
        alpha = jnp.exp(m_old - m_new)
        p = jnp.where(mask, jnp.exp(s - m_new), 0.0)
        l_scr[...] = alpha * l_scr[...] + jnp.sum(p, axis=-1, keepdims=True)
        pb = p.astype(BF16)
        pv = jnp.concatenate([jnp.dot(pb[h * 2 * tq:(h + 1) * 2 * tq], v_tile(h).astype(BF16),
                                      preferred_element_type=F32) for h in range(DA_HEADS)], axis=0)
        acc_scr[...] = alpha * acc_scr[...] + pv
        m_scr[...] = m_new

    @pl.when(c_idx < n_chunks)
    def _():
        per_pos = 2 * DA_HEADS
        n_seq = pl.num_programs(0)
        chunk_rows = pages_per_chunk * page_size

        def chunk_copies(seq, chunk, slot):
            return _page_copies(cache_ref, e, pt_ref, seq, chunk * pages_per_chunk, pages_per_chunk,
                                0, LANES, bufs[slot], sems[slot], page_size * per_pos)

        def run(slot):
            @pl.when((s_idx == 0) & (c_idx == 0))
            def _():
                for cp in chunk_copies(s_idx, c_idx, slot):
                    cp.start()

            @pl.when(c_idx + 1 < n_chunks)
            def _():
                for cp in chunk_copies(s_idx, c_idx + 1, 1 - slot):
                    cp.start()

            @pl.when((c_idx + 1 == n_chunks) & (s_idx + 1 < n_seq))
            def _():
                for cp in chunk_copies(s_idx + 1, 0, 1 - slot):
                    cp.start()

            for cp in chunk_copies(s_idx, c_idx, slot):
                cp.wait()
            buf = bufs[slot]
            for r0 in range(0, chunk_rows, tk):
                kpos = (c_idx * chunk_rows + r0) + lax.broadcasted_iota(jnp.int32, (1, tk), 1)
                update(lambda h: buf[pl.ds(r0 * per_pos + h, tk, stride=per_pos), :],
                       lambda h: buf[pl.ds(r0 * per_pos + DA_HEADS + h, tk, stride=per_pos), :], kpos)

        parity = (s_idx * n_chunks + c_idx) & 1
        for slot in range(2):
            pl.when(parity == slot)(functools.partial(run, slot))

    @pl.when(c_idx == n_chunks)
    def _():
        n_tail = tail_ref.shape[0]
        kpos = past_len + lax.broadcasted_iota(jnp.int32, (1, n_tail), 1)
        lam = _diff_lambda(lp_ref, lam_init)
        update(lambda h: tail_ref[:, h * LANES:(h + 1) * LANES],
               lambda h: tail_ref[:, width + h * LANES:width + (h + 1) * LANES], kpos)
        for h in range(DA_HEADS):
            hs = slice(h * 2 * tq, (h + 1) * 2 * tq)
            o_ref[:, h * LANES:(h + 1) * LANES] = _diff_finish(
                m_scr.at[hs], l_scr.at[hs], acc_scr.at[hs], lam, sg_ref[...], lam_init, tq)


def _diff_sample(projb_s, tail, cache4, e, page_table, lam_par, subln, lam_init, past_len):
    s, tq, _ = projb_s.shape
    n_pages = page_table.shape[1]
    page_size = cache4.shape[2] // (2 * DA_HEADS)
    pages_per_chunk = math.gcd(n_pages, 16)
    n_chunks = n_pages // pages_per_chunk
    width = DA_HEADS * DA_VDIM
    tk = math.gcd(pages_per_chunk * page_size, 512)
    kern = functools.partial(_diff_sample_kernel, e=e, n_chunks=n_chunks, pages_per_chunk=pages_per_chunk,
                             page_size=page_size, past_len=past_len, tk=tk, lam_init=lam_init)
    grid_spec = pltpu.PrefetchScalarGridSpec(
        num_scalar_prefetch=1,
        grid=(s, n_chunks + 1),
        in_specs=[
            pl.BlockSpec((None, tq, width), lambda si, ci, pt: (si, 0, 0)),
            pl.BlockSpec((None, tail.shape[1], 2 * width), lambda si, ci, pt: (si, 0, 0)),
            pl.BlockSpec((4, HEAD_DIM), lambda si, ci, pt: (0, 0)),
            pl.BlockSpec((1, DA_VDIM), lambda si, ci, pt: (0, 0)),
            pl.BlockSpec(memory_space=pl.ANY),
        ],
        out_specs=pl.BlockSpec((None, tq, width), lambda si, ci, pt: (si, 0, 0)),
        scratch_shapes=[
            pltpu.VMEM((pages_per_chunk * page_size * 2 * DA_HEADS, LANES), F32),
            pltpu.VMEM((pages_per_chunk * page_size * 2 * DA_HEADS, LANES), F32),
            pltpu.SemaphoreType.DMA((pages_per_chunk,)),
            pltpu.SemaphoreType.DMA((pages_per_chunk,)),
            pltpu.VMEM((DA_HEADS * 2 * tq, 1), F32),
            pltpu.VMEM((DA_HEADS * 2 * tq, 1), F32),
            pltpu.VMEM((DA_HEADS * 2 * tq, LANES), F32),
        ],
    )
    return pl.pallas_call(
        kern,
        grid_spec=grid_spec,
        out_shape=jax.ShapeDtypeStruct((s, tq, width), F32),
        compiler_params=_cparams(("arbitrary", "arbitrary")),
    )(page_table, projb_s, tail, lam_par, subln.reshape(1, DA_VDIM), cache4)


def _compress_compute(xk_ref, xv_ref, pe_ref, w1_ref, w2_ref, kg_ref, kcb_ref, vcb_ref, nb):
    def body(l, accs):
        pe = pe_ref[pl.ds(l, 1), :]
        ak = (xk_ref[pl.ds(l, nb, stride=NSA_BLOCK), :] + pe[:, 0:LANES]).astype(BF16)
        av = (xv_ref[pl.ds(l, nb, stride=NSA_BLOCK), :] + pe[:, LANES:2 * LANES]).astype(BF16)
        hk = jnp.dot(ak, w1_ref[0, l], preferred_element_type=F32)
        hv = jnp.dot(av, w1_ref[1, l], preferred_element_type=F32)
        return accs[0] + hk, accs[1] + hv

    zero = jnp.zeros((nb, 2 * NSA_CMP_HID), F32)
    hk, hv = lax.fori_loop(0, NSA_BLOCK, body, (zero, zero))
    ck = jnp.dot(jax.nn.gelu(hk).astype(BF16), w2_ref[0], preferred_element_type=F32)
    cv = jnp.dot(jax.nn.gelu(hv).astype(BF16), w2_ref[1], preferred_element_type=F32)
    ck = _seg_rms(ck, kg_ref[...])
    for ref in (kcb_ref, vcb_ref):
        ref[...] = jnp.zeros(ref.shape, ref.dtype)
    kcb_ref[0:nb, :] = ck.astype(BF16)
    vcb_ref[0:nb, :] = cv.astype(BF16)


def _compress_prompt_kernel(xk_ref, xv_ref, pe_ref, w1_ref, w2_ref, kg_ref, kcb_ref, vcb_ref, *, nb):
    _compress_compute(xk_ref, xv_ref, pe_ref, w1_ref, w2_ref, kg_ref, kcb_ref, vcb_ref, nb)


def _compress_specs(nbp):
    hid2 = 2 * NSA_CMP_HID
    w_specs = [
        pl.BlockSpec((NSA_BLOCK, 2 * LANES), lambda *a: (0, 0)),
        pl.BlockSpec((2, NSA_BLOCK, LANES, hid2), lambda *a: (0, 0, 0, 0)),
        pl.BlockSpec((2, hid2, LANES), lambda *a: (0, 0, 0)),
        pl.BlockSpec((1, LANES), lambda *a: (0, 0)),
    ]
    out_specs = [pl.BlockSpec((None, nbp, LANES), lambda bi, *a: (bi, 0, 0))] * 2
    return w_specs, out_specs


def _compress_out_shape(nbat, nbp):
    return [jax.ShapeDtypeStruct((nbat, nbp, LANES), BF16)] * 2


def _compress_prompt(projf, cw):
    b, t, _ = projf.shape
    nb = t // NSA_BLOCK
    nbp = -(-nb // LANES) * LANES
    w_specs, out_specs = _compress_specs(nbp)
    return pl.pallas_call(
        functools.partial(_compress_prompt_kernel, nb=nb),
        grid=(b,),
        in_specs=[pl.BlockSpec((None, t, LANES), lambda bi: (bi, 0, C_NKV // LANES)),
                  pl.BlockSpec((None, t, LANES), lambda bi: (bi, 0, C_NKV // LANES + 1))] + w_specs,
        out_specs=out_specs,
        out_shape=_compress_out_shape(b, nbp),
        compiler_params=_cparams(("parallel",)),
    )(projf, projf, *cw)


def _feature_page_copies(cache_ref, e, pt_ref, seq, n_pages, row0, n_rows, buf, sem):
    copies = []
    for j in range(n_pages):
        copies.append(pltpu.make_async_copy(
            cache_ref.at[e, pt_ref[seq, j], pl.ds(row0, n_rows), :],
            buf.at[pl.ds(j * n_rows, n_rows), :],
            sem.at[j]))
    return copies


def _per_sequence_pages(cache_ref, e, pt_ref, n_pages, row0, n_rows, bufs, sems, use):
    s_idx = pl.program_id(0)

    def run(slot):
        def copies(seq, which):
            return _feature_page_copies(cache_ref, e, pt_ref, seq, n_pages, row0, n_rows, bufs[which], sems[which])

        @pl.when(s_idx == 0)
        def _():
            for cp in copies(s_idx, slot):
                cp.start()

        @pl.when(s_idx + 1 < pl.num_programs(0))
        def _():
            for cp in copies(s_idx + 1, 1 - slot):
                cp.start()

        for cp in copies(s_idx, slot):
            cp.wait()
        use(bufs[slot])

    for slot in range(2):
        pl.when((s_idx & 1) == slot)(functools.partial(run, slot))


def _compress_sample_kernel(pt_ref, tail_ref, pe_ref, w1_ref, w2_ref, kg_ref, cache_ref,
                            kcb_ref, vcb_ref, buf0, buf1, sem0, sem1, *, e, n_pages, half_stride):
    feat = 2 * LANES
    compute = functools.partial(_compress_pages, tail_ref, pe_ref, w1_ref, w2_ref, kg_ref, kcb_ref, vcb_ref,
                                n_pages=n_pages, half_stride=half_stride)
    _per_sequence_pages(cache_ref, e, pt_ref, n_pages, 0, feat, (buf0, buf1), (sem0, sem1), compute)


def _compress_pages(tail_ref, pe_ref, w1_ref, w2_ref, kg_ref, kcb_ref, vcb_ref, buf, *, n_pages, half_stride):
    feat = 2 * LANES
    buf[n_pages * feat:, :] = tail_ref[...]
    n_pp = n_pages + 1

    def body(d, accs):
        out = []
        for c in range(2):
            for g in range(NSA_GROUPS):
                f = c * LANES + g * HEAD_DIM + d
                a = (buf[pl.ds(f, n_pp, stride=feat), :] + pe_ref[pl.ds(f, 1), :]).astype(BF16)
                out.append(accs[c * NSA_GROUPS + g] + jnp.dot(a, w1_ref[c, d], preferred_element_type=F32))
        return tuple(out)

    zero = jnp.zeros((n_pp, 2 * NSA_CMP_HID), F32)
    accs = lax.fori_loop(0, HEAD_DIM, body, (zero,) * (2 * NSA_GROUPS))
    pad = jnp.zeros((-n_pp % (2 * SUBLANES), LANES), F32)
    for c, out_ref in ((0, kcb_ref), (1, vcb_ref)):
        out_ref[...] = jnp.zeros(out_ref.shape, out_ref.dtype)
        for half in range(2):
            hs = slice(half * NSA_CMP_HID, (half + 1) * NSA_CMP_HID)
            hid = jnp.concatenate([accs[c * NSA_GROUPS + g][:, hs] for g in range(NSA_GROUPS)], axis=1)
            y = jnp.dot(jax.nn.gelu(hid).astype(BF16), w2_ref[c], preferred_element_type=F32)
            if c == 0:
                y = _seg_rms(y, kg_ref[...])
            y = jnp.concatenate([y, pad], axis=0).astype(BF16)
            out_ref[half * half_stride:half * half_stride + y.shape[0], :] = y


def _compress_sample(tail_t, cache_t, e, page_table, cw_t):
    s = tail_t.shape[0]
    n_pages = page_table.shape[1]
    page_size = cache_t.shape[3]
    assert page_size == 2 * NSA_BLOCK
    feat = 2 * LANES
    half_stride = -(-(n_pages + 1) // LANES) * LANES
    nbp = 2 * half_stride
    hid2 = 2 * NSA_CMP_HID
    grid_spec = pltpu.PrefetchScalarGridSpec(
        num_scalar_prefetch=1,
        grid=(s,),
        in_specs=[
            pl.BlockSpec((None, feat, page_size), lambda si, pt: (si, 0, 0)),
            pl.BlockSpec((feat, page_size), lambda si, pt: (0, 0)),
            pl.BlockSpec((2, HEAD_DIM, LANES, hid2), lambda si, pt: (0, 0, 0, 0)),
            pl.BlockSpec((2, hid2, LANES), lambda si, pt: (0, 0, 0)),
            pl.BlockSpec((1, LANES), lambda si, pt: (0, 0)),
            pl.BlockSpec(memory_space=pl.ANY),
        ],
        out_specs=[pl.BlockSpec((None, nbp, LANES), lambda si, pt: (si, 0, 0))] * 2,
        scratch_shapes=[pltpu.VMEM(((n_pages + 1) * feat, page_size), F32)] * 2
        + [pltpu.SemaphoreType.DMA((n_pages,))] * 2,
    )
    return pl.pallas_call(
        functools.partial(_compress_sample_kernel, e=e, n_pages=n_pages, half_stride=half_stride),
        grid_spec=grid_spec,
        out_shape=_compress_out_shape(s, nbp),
        compiler_params=_cparams(("arbitrary",)),
    )(page_table, tail_t, *cw_t, cache_t)


def _nsa_branches(q, kc_ref, vc_ref, k_sel_tile, vt_sel_tile, k_win_tile, vt_win_tile,
                  sel_scr, m_s, l_s, a_s, m_w, l_w, a_w,
                  *, tq, q0, n_blk, n_sel_rows, tk, wbase, n_win_rows, tkw,
                  row_block=lambda row: row, block_row=lambda blk: blk):
    nh = NSA_HEADS
    n_rows = nh * tq
    blk_shift = NSA_BLOCK.bit_length() - 1
    tq_shift = tq.bit_length() - 1
    assert 1 << tq_shift == tq and 1 << blk_shift == NSA_BLOCK
    lane128 = lax.broadcasted_iota(jnp.int32, (tq, LANES), 1)
    qf = q.astype(F32) * (HEAD_DIM ** -0.5 * LOG2E)
    parts = []
    for hh in range(nh):
        g = hh // NSA_REP
        blk = qf[:, (hh // 2) * LANES:(hh // 2 + 1) * LANES]
        if hh % 2 != g:
            blk = pltpu.roll(blk, HEAD_DIM, 1)
        in_seg = (lane128 >= g * HEAD_DIM) & (lane128 < (g + 1) * HEAD_DIM)
        parts.append(jnp.where(in_seg, blk, 0.0))
    qst = jnp.concatenate(parts, axis=0).T.astype(BF16)

    lane = lax.broadcasted_iota(jnp.int32, (1, n_rows), 1)
    qpos = q0 + (lane & (tq - 1))

    nbp = kc_ref.shape[0]
    nb8 = sel_scr.shape[0]
    assert n_blk <= nb8 <= nbp
    sc = jnp.dot(kc_ref[...], qst, preferred_element_type=F32)
    blk_p = row_block(lax.broadcasted_iota(jnp.int32, (nbp, 1), 0))
    cmp_ok = blk_p < ((qpos + 1) >> blk_shift)
    sc = jnp.where(cmp_ok, sc, NEG)
    mc = jnp.max(sc, axis=0, keepdims=True)
    ec = jnp.where(cmp_ok, jnp.exp2(sc - mc), 0.0)
    pc_all = ec / jnp.maximum(jnp.sum(ec, axis=0, keepdims=True), 1e-30)
    vct = vc_ref[...].astype(F32).T.astype(BF16)
    o_cmp = jnp.dot(vct, pc_all.astype(BF16), preferred_element_type=F32)
    pc = pc_all[0:nb8]
    blk_i = blk_p[0:nb8]
    blk_f = blk_i.astype(F32)

    n_pick = min(NSA_TOPK - 1, n_blk)

    def pick(imp, cur):
        cand = blk_i < cur
        x = jnp.where(cand, imp, -1.0)
        x = jnp.where(blk_i < n_blk, x, -2.0)
        picked = jnp.zeros(x.shape, F32)
        for _ in range(n_pick):
            mx = jnp.max(x, axis=0, keepdims=True)
            first = jnp.min(jnp.where(x == mx, blk_f, 3e38), axis=0, keepdims=True)
            hit = blk_f == first
            picked = jnp.where(hit, 1.0, picked)
            x = jnp.where(hit, -3.0, x)
        chosen = (blk_i == cur) | (cand & (picked > 0.5))
        return jnp.where(chosen, 0.0, NEG)

    if tq % LANES == 0:
        cur_t = (q0 + lax.broadcasted_iota(jnp.int32, (1, tq), 1)) >> blk_shift
        for g in range(NSA_GROUPS):
            base = g * NSA_REP * tq
            imp = pc[:, base:base + tq]
            for r in range(1, NSA_REP):
                imp = imp + pc[:, base + r * tq:base + (r + 1) * tq]
            sel_g = pick(imp, cur_t)
            for r in range(NSA_REP):
                sel_scr[:, base + r * tq:base + (r + 1) * tq] = sel_g
    else:
        assert n_rows == LANES
        rep = (lane >> tq_shift) & (NSA_REP - 1)
        imp = pc
        for d in range(1, NSA_REP):
            up = pltpu.roll(pc, LANES - d * tq, 1)
            dn = pltpu.roll(pc, d * tq, 1)
            imp = imp + jnp.where(rep + d < NSA_REP, up, 0.0) + jnp.where(rep >= d, dn, 0.0)
        sel_scr[...] = pick(imp, qpos >> blk_shift)

    _flash_init_t(m_s, l_s, a_s)
    blocks_per_tile = tk // NSA_BLOCK

    def sel_tile(ki, causal):
        k0 = pl.multiple_of(ki * tk, tk)
        st = jnp.dot(k_sel_tile(k0), qst, preferred_element_type=F32)
        b0 = ki * blocks_per_tile
        kpos = k0 + lax.broadcasted_iota(jnp.int32, (tk, 1), 0)
        pieces = []
        for bb in range(blocks_per_tile):
            rs = slice(bb * NSA_BLOCK, (bb + 1) * NSA_BLOCK)
            piece = st[rs] + sel_scr[pl.ds(block_row(b0 + bb), 1), :]
            pieces.append(jnp.where(kpos[rs] <= qpos, piece, NEG) if causal else piece)
        _flash_update_t(jnp.concatenate(pieces, axis=0), vt_sel_tile(k0), m_s, l_s, a_s)

    def sel_full(ki, carry):
        sel_tile(ki, False)
        return carry

    def sel_diag(ki, carry):
        sel_tile(ki, True)
        return carry

    n_sel = jnp.minimum((q0 + tq - 1) // tk, n_sel_rows // tk - 1) + 1
    n_full = jnp.minimum(q0 // tk, n_sel)
    lax.fori_loop(0, n_full, sel_full, 0)
    lax.fori_loop(n_full, n_sel, sel_diag, 0)

    _flash_init_t(m_w, l_w, a_w)

    def win_body(ki, carry):
        k0 = pl.multiple_of(ki * tkw, tkw)
        st = jnp.dot(k_win_tile(k0), qst, preferred_element_type=F32)
        kpos = wbase + k0 + lax.broadcasted_iota(jnp.int32, (tkw, 1), 0)
        kpos = jnp.where(kpos >= 0, kpos, jnp.iinfo(jnp.int32).max)
        dist = qpos - kpos
        ok = (dist >= 0) & (dist <= NSA_WINDOW)
        _flash_update_t(jnp.where(ok, st, NEG), vt_win_tile(k0), m_w, l_w, a_w)
        return carry

    lo_tile = jnp.maximum(q0 - NSA_WINDOW - wbase, 0) // tkw
    hi_tile = jnp.minimum((q0 + tq - 1 - wbase) // tkw, n_win_rows // tkw - 1)
    lax.fori_loop(lo_tile, hi_tile + 1, win_body, 0)

    o_sel = a_s[...] / jnp.maximum(l_s[...], 1e-30)
    o_win = a_w[...] / jnp.maximum(l_w[...], 1e-30)
    return o_cmp, o_sel, o_win


def _nsa_gate_lanes(o_cmp, o_sel, o_win, gate_logits, tq):
    gt = jax.nn.sigmoid(gate_logits).T
    def gate_row(branch):
        return jnp.concatenate([gt[3 * hh + branch:3 * hh + branch + 1, :] for hh in range(NSA_HEADS)], axis=1)
    ot = gate_row(0) * o_cmp + gate_row(1) * o_sel + gate_row(2) * o_win
    outs = []
    for j in range(NSA_HEADS // 2):
        g = (2 * j) // NSA_REP
        rs = slice(g * HEAD_DIM, (g + 1) * HEAD_DIM)
        pair = jnp.concatenate([ot[rs, (2 * j) * tq:(2 * j + 1) * tq],
                                ot[rs, (2 * j + 1) * tq:(2 * j + 2) * tq]], axis=0)
        outs.append(pair.T)
    return jnp.concatenate(outs, axis=1)


def _nsa_gate_rows(o_cmp, o_sel, o_win, gate_logits, tq):
    oc, os_, ow = o_cmp.T, o_sel.T, o_win.T
    gates = jax.nn.sigmoid(gate_logits)
    lo_half = lax.broadcasted_iota(jnp.int32, (tq, LANES), 1) < HEAD_DIM
    heads = []
    for hh in range(NSA_HEADS):
        g = hh // NSA_REP
        rs = slice(hh * tq, (hh + 1) * tq)
        o = (gates[:, 3 * hh:3 * hh + 1] * oc[rs] + gates[:, 3 * hh + 1:3 * hh + 2] * os_[rs]
             + gates[:, 3 * hh + 2:3 * hh + 3] * ow[rs])
        if hh % 2 != g:
            o = pltpu.roll(o, HEAD_DIM, 1)
        heads.append(o)
    return jnp.concatenate([jnp.where(lo_half, heads[2 * j], heads[2 * j + 1])
                            for j in range(NSA_HEADS // 2)], axis=1)


def _nsa_scratch_t(tq, nb8):
    n_rows = NSA_HEADS * tq
    one = [pltpu.VMEM((1, n_rows), F32), pltpu.VMEM((1, n_rows), F32), pltpu.VMEM((LANES, n_rows), F32)]
    return [pltpu.VMEM((nb8, n_rows), F32)] + one + one


def _nsa_prompt_kernel_t(q_ref, g_ref, kc_ref, vc_ref, ks_ref, vst_ref, kw_ref, vwt_ref, o_ref,
                         sel_scr, m_s, l_s, a_s, m_w, l_w, a_w, *, tq, tk, tkw, n_blk, t):
    q0 = pl.program_id(1) * tq
    branches = _nsa_branches(
        q_ref[...], kc_ref, vc_ref,
        lambda k0: ks_ref[pl.ds(k0, tk), :], lambda k0: vst_ref[:, pl.ds(k0, tk)],
        lambda k0: kw_ref[pl.ds(k0, tkw), :], lambda k0: vwt_ref[:, pl.ds(k0, tkw)],
        sel_scr, m_s, l_s, a_s, m_w, l_w, a_w,
        tq=tq, q0=q0, n_blk=n_blk, n_sel_rows=t, tk=tk, wbase=0, n_win_rows=t, tkw=tkw)
    o_ref[...] = _nsa_gate_lanes(*branches, g_ref[...], tq)


def _nsa_prompt_t(projf, projb, vst, vwt, kcb, vcb, tq, tk):
    b, t, _ = projb.shape
    nbp = kcb.shape[1]
    n_blk = t // NSA_BLOCK
    nb8 = -(-n_blk // (2 * SUBLANES)) * (2 * SUBLANES)
    tkw = min(tq, 2 * LANES)
    kern = functools.partial(_nsa_prompt_kernel_t, tq=tq, tk=tk, tkw=tkw, n_blk=n_blk, t=t)
    ks_blk = (C_NKV + 2 * LANES) // LANES
    kw_blk = (C_NKV + 4 * LANES) // LANES
    return pl.pallas_call(
        kern,
        grid=(b, t // tq),
        in_specs=[
            pl.BlockSpec((None, tq, 512), lambda bi, qi: (bi, qi, C_NQ // 512)),
            pl.BlockSpec((None, tq, LANES), lambda bi, qi: (bi, qi, C_NG // LANES)),
            pl.BlockSpec((None, nbp, LANES), lambda bi, qi: (bi, 0, 0)),
            pl.BlockSpec((None, nbp, LANES), lambda bi, qi: (bi, 0, 0)),
            pl.BlockSpec((None, t, LANES), lambda bi, qi: (bi, 0, ks_blk)),
            pl.BlockSpec((None, LANES, t), lambda bi, qi: (bi, 0, 0)),
            pl.BlockSpec((None, t, LANES), lambda bi, qi: (bi, 0, kw_blk)),
            pl.BlockSpec((None, LANES, t), lambda bi, qi: (bi, 0, 0)),
        ],
        out_specs=pl.BlockSpec((None, tq, 512), lambda bi, qi: (bi, qi, 0)),
        out_shape=jax.ShapeDtypeStruct((b, t, NSA_HEADS * HEAD_DIM), F32),
        scratch_shapes=_nsa_scratch_t(tq, nb8),
        compiler_params=_cparams(("parallel", "arbitrary")),
    )(projb, projf, kcb, vcb, projb, vst, projb, vwt)


def _nsa_sample_kernel_t(pt_ref, q_ref, g_ref, kc_ref, vc_ref, tail_ref, kwv_ref, cache_ref, o_ref,
                         buf0, buf1, sem0, sem1, sel_scr, m_s, l_s, a_s, m_w, l_w, a_w,
                         *, e, n_pages, page_size, past_len, n_blk, tkw, wbase, half_stride, pages_per_tile):
    feat = 2 * LANES
    compute = functools.partial(
        _nsa_sample_pages, q_ref, g_ref, kc_ref, vc_ref, tail_ref, kwv_ref, o_ref,
        sel_scr, m_s, l_s, a_s, m_w, l_w, a_w, n_pages=n_pages, page_size=page_size, past_len=past_len,
        n_blk=n_blk, tkw=tkw, wbase=wbase, half_stride=half_stride, pages_per_tile=pages_per_tile)
    _per_sequence_pages(cache_ref, e, pt_ref, n_pages, feat, feat, (buf0, buf1), (sem0, sem1), compute)


def _nsa_sample_pages(q_ref, g_ref, kc_ref, vc_ref, tail_ref, kwv_ref, o_ref,
                      sel_scr, m_s, l_s, a_s, m_w, l_w, a_w, buf,
                      *, n_pages, page_size, past_len, n_blk, tkw, wbase, half_stride, pages_per_tile):
    feat = 2 * LANES
    buf[n_pages * feat:, :] = tail_ref[...]
    tq = q_ref.shape[0]
    tk = pages_per_tile * page_size

    def page_rows(k0, j, first):
        return pl.ds(pl.multiple_of((k0 // page_size + j) * feat + first, LANES), LANES)

    def k_tile(k0):
        return jnp.concatenate([buf[page_rows(k0, j, 0), :].T for j in range(pages_per_tile)],
                               axis=0).astype(BF16)

    def vt_tile(k0):
        return jnp.concatenate([buf[page_rows(k0, j, LANES), :] for j in range(pages_per_tile)],
                               axis=1).astype(BF16)

    def row_block(row):
        half = (row >= half_stride).astype(jnp.int32)
        page = row - half * half_stride
        return jnp.where(page <= n_pages, 2 * page + half, jnp.iinfo(jnp.int32).max)

    branches = _nsa_branches(
        q_ref[...], kc_ref, vc_ref,
        k_tile, vt_tile,
        lambda k0: kwv_ref[pl.ds(k0, tkw), 0:LANES].astype(BF16),
        lambda k0: kwv_ref[pl.ds(k0, tkw), LANES:2 * LANES].T.astype(BF16),
        sel_scr, m_s, l_s, a_s, m_w, l_w, a_w,
        tq=tq, q0=past_len, n_blk=n_blk, n_sel_rows=(n_pages + 1) * page_size, tk=tk,
        wbase=wbase, n_win_rows=kwv_ref.shape[0], tkw=tkw,
        row_block=row_block, block_row=lambda blk: (blk & 1) * half_stride + (blk >> 1))
    o_ref[...] = _nsa_gate_rows(*branches, g_ref[...], tq)


def _nsa_sample_t(projf_s, projb_s, kcb, vcb, sel_tail_t, kwv, cache_t, e, page_table, past_len, n_valid):
    s, tq, _ = projb_s.shape
    n_pages = page_table.shape[1]
    page_size = cache_t.shape[3]
    feat = 2 * LANES
    nbp = kcb.shape[1]
    half_stride = nbp // 2
    n_blk = -(-(past_len + n_valid) // NSA_BLOCK)
    nb8 = nbp
    tkw = math.gcd(kwv.shape[1], LANES)
    wbase = past_len - (kwv.shape[1] - page_size)
    assert page_size == LANES
    pages_per_tile = max(p for p in range(1, 9) if (n_pages + 1) % p == 0)
    kern = functools.partial(_nsa_sample_kernel_t, e=e, n_pages=n_pages, page_size=page_size,
                             past_len=past_len, n_blk=n_blk, tkw=tkw, wbase=wbase, half_stride=half_stride,
                             pages_per_tile=pages_per_tile)
    grid_spec = pltpu.PrefetchScalarGridSpec(
        num_scalar_prefetch=1,
        grid=(s,),
        in_specs=[
            pl.BlockSpec((None, tq, 512), lambda si, pt: (si, 0, C_NQ // 512)),
            pl.BlockSpec((None, tq, LANES), lambda si, pt: (si, 0, C_NG // LANES)),
            pl.BlockSpec((None, nbp, LANES), lambda si, pt: (si, 0, 0)),
            pl.BlockSpec((None, nbp, LANES), lambda si, pt: (si, 0, 0)),
            pl.BlockSpec((None, feat, page_size), lambda si, pt: (si, 0, 0)),
            pl.BlockSpec((None, kwv.shape[1], 2 * LANES), lambda si, pt: (si, 0, 0)),
            pl.BlockSpec(memory_space=pl.ANY),
        ],
        out_specs=pl.BlockSpec((None, tq, 512), lambda si, pt: (si, 0, 0)),
        scratch_shapes=[pltpu.VMEM(((n_pages + 1) * feat, page_size), F32)] * 2
        + [pltpu.SemaphoreType.DMA((n_pages,))] * 2 + _nsa_scratch_t(tq, nb8),
    )
    return pl.pallas_call(
        kern,
        grid_spec=grid_spec,
        out_shape=jax.ShapeDtypeStruct((s, tq, NSA_HEADS * HEAD_DIM), F32),
        compiler_params=_cparams(("arbitrary",)),
    )(page_table, projb_s, projf_s, kcb, vcb, sel_tail_t, kwv, cache_t)


def _split3(x):
    hi = x.astype(BF16)
    r1 = x - hi.astype(F32)
    mid = r1.astype(BF16)
    lo = (r1 - mid.astype(F32)).astype(BF16)
    return hi, mid, lo


def _hgrn_kernel(q_ref, f_ref, i_ref, g_ref, lb_ref, ng_ref, s0_ref, o_ref, sfin_ref,
                 st_scr, cum_scr, k_scr, v_scr, *, chunk, n_chunks, n_valid):
    ti = pl.program_id(2)

    @pl.when(ti == 0)
    def _():
        st_scr[...] = s0_ref[...]

    lb = lb_ref[...]
    tri = (lax.broadcasted_iota(jnp.int32, (chunk, chunk), 0)
           >= lax.broadcasted_iota(jnp.int32, (chunk, chunk), 1)).astype(BF16)
    row = lax.broadcasted_iota(jnp.int32, (chunk, 1), 0)
    row8 = lax.broadcasted_iota(jnp.int32, (SUBLANES, 1), 0)
    n_sub = chunk // SUBLANES

    def chunk_body(c, carry):
        r0 = pl.multiple_of(c * chunk, chunk)
        qr = q_ref[pl.ds(r0, chunk), :]
        q = qr * jax.nn.sigmoid(qr) * (HG_DK ** -0.5)
        fg = lb + (1.0 - lb) * jax.nn.sigmoid(f_ref[pl.ds(r0, chunk), :])
        logf = jnp.log(fg)
        k = 1.0 - fg
        v = i_ref[pl.ds(r0, chunk), :]
        if n_valid is not None:
            live = (ti * (chunk * n_chunks) + r0 + row) < n_valid
            logf = jnp.where(live, logf, 0.0)
            k = jnp.where(live, k, 0.0)
        cum = sum(jnp.dot(tri, part, preferred_element_type=F32) for part in _split3(logf))
        cum2 = cum * LOG2E
        cum_scr[...] = cum2
        k_scr[...] = k
        v_scr[...] = v
        st = st_scr[...]
        inter = _nt_dot((q * jnp.exp2(cum2)).astype(BF16), st.astype(BF16))

        q_sub = [q[i * SUBLANES:(i + 1) * SUBLANES] for i in range(n_sub)]
        c_sub = [cum2[i * SUBLANES:(i + 1) * SUBLANES] for i in range(n_sub)]
        prods = []
        for s in range(chunk):
            j = s // SUBLANES
            cs = jnp.broadcast_to(cum_scr[s:s + 1, :], (SUBLANES, HG_DK))
            ks = jnp.broadcast_to(k_scr[s:s + 1, :], (SUBLANES, HG_DK))
            for i in range(j, n_sub):
                dec = jnp.exp2(c_sub[i] - cs)
                if i == j:
                    dec = jnp.where(row8 >= s - j * SUBLANES, dec, 0.0)
                prods.append(q_sub[i] * ks * dec)
        sums = jnp.dot(jnp.concatenate(prods, axis=0).astype(BF16), jnp.ones((HG_DK, HG_DK), BF16),
                       preferred_element_type=F32)
        o_sub = [inter[i * SUBLANES:(i + 1) * SUBLANES] for i in range(n_sub)]
        n = 0
        for s in range(chunk):
            vs = jnp.broadcast_to(v_scr[s:s + 1, :], (SUBLANES, HG_DK))
            for i in range(s // SUBLANES, n_sub):
                o_sub[i] = o_sub[i] + sums[n * SUBLANES:(n + 1) * SUBLANES] * vs
                n += 1
        o = jnp.concatenate(o_sub, axis=0)

        last = cum2[chunk - 1:chunk, :]
        kd = k * jnp.exp2(last - cum2)
        st_scr[...] = st * jnp.exp2(last) + jnp.dot(v.T.astype(BF16), kd.astype(BF16),
                                                     preferred_element_type=F32)
        gr = g_ref[pl.ds(r0, chunk), :]
        o_ref[pl.ds(r0, chunk), :] = _rms(o, ng_ref[...]) * (gr * jax.nn.sigmoid(gr))
        return carry

    lax.fori_loop(0, n_chunks, chunk_body, 0, unroll=math.gcd(n_chunks, 4))

    @pl.when(ti == pl.num_programs(2) - 1)
    def _():
        sfin_ref[...] = st_scr[...]


def _hgrn(projf, lb, norm_gain, s0t, chunk, tt, n_valid):
    b, t, _ = projf.shape
    dk = HG_DK
    kern = functools.partial(_hgrn_kernel, chunk=chunk, n_chunks=tt // chunk, n_valid=n_valid)
    col = lambda j: pl.BlockSpec((None, tt, dk), lambda bi, h, ti: (bi, ti, j * HG_HEADS + h))
    vec = pl.BlockSpec((1, dk), lambda bi, h, ti: (0, h))
    st_spec = pl.BlockSpec((None, None, dk, dk), lambda bi, h, ti: (bi, h, 0, 0))
    return pl.pallas_call(
        kern,
        grid=(b, HG_HEADS, t // tt),
        in_specs=[col(0), col(1), col(2), col(3), vec, vec, st_spec],
        out_specs=[pl.BlockSpec((None, tt, dk), lambda bi, h, ti: (bi, ti, h)), st_spec],
        out_shape=[jax.ShapeDtypeStruct((b, t, HG_HEADS * dk), F32),
                   jax.ShapeDtypeStruct((b, HG_HEADS, dk, dk), F32)],
        scratch_shapes=[pltpu.VMEM((dk, dk), F32)] + [pltpu.VMEM((chunk, dk), F32)] * 3,
        compiler_params=_cparams(("parallel", "parallel", "arbitrary")),
    )(projf, projf, projf, projf, lb.reshape(1, -1), norm_gain.reshape(1, -1), s0t)


def _row_tile(n, pref):
    return math.gcd(n, pref)


def _pad_rows(x, rows):
    return jnp.pad(x, ((0, 0), (0, rows - x.shape[1]), (0, 0)))


def _even_weights(w_in, qk_a, qk_b, pe, w1, w2):
    w_pad = jnp.pad(w_in, ((0, 0), (0, EVEN_IN_PAD - EVEN_IN))).astype(BF16)
    ones = jnp.ones((HEAD_DIM,), F32)
    segs = [qk_a[0]] * 8 + [qk_a[1]] * 8 + [ones] * 8 + [qk_b[0]] * 8 + [ones] * 4 + [qk_b[2]] * 2 \
        + [ones] * 2 + [qk_b[3]] * 2 + [ones] * 4
    colgain = jnp.concatenate(segs).reshape(1, EVEN_IN_PAD).astype(F32)
    pe_cat = jnp.concatenate([pe[0], pe[0], pe[1], pe[1]], axis=-1).astype(F32)
    z1 = jnp.zeros_like(w1)
    w1bd = jnp.concatenate([jnp.concatenate([w1, z1], axis=-1), jnp.concatenate([z1, w1], axis=-1)],
                           axis=-2).astype(BF16)
    z2 = jnp.zeros_like(w2)
    w2bd = jnp.concatenate([jnp.concatenate([w2, z2], axis=-1), jnp.concatenate([z2, w2], axis=-1)],
                           axis=-2).astype(BF16)
    kgain = jnp.concatenate([qk_b[1], qk_b[1]]).reshape(1, LANES).astype(F32)
    pe_dl = jnp.swapaxes(pe, 1, 2)
    pe_dl = jnp.concatenate([pe_dl, pe_dl], axis=-1)
    pe_t = jnp.stack([pe_dl] * NSA_GROUPS, axis=1).reshape(2 * LANES, LANES).astype(F32)
    w1_dl = jnp.swapaxes(w1, 1, 2)
    z1t = jnp.zeros_like(w1_dl)
    w1t = jnp.concatenate([jnp.concatenate([w1_dl, z1t], axis=-1), jnp.concatenate([z1t, w1_dl], axis=-1)],
                          axis=-2).astype(BF16)
    return w_pad, colgain, (pe_cat, w1bd, w2bd, kgain), (pe_t, w1t, w2bd, kgain)


def kernel(x_prompt, x_sample, cache_diff_kv, cache_nsa_kv, cache_nsa_win, state_hgrn, page_table,
           norm_mix, norm_ffn, w_ffn_in, w_ffn_out, w_in_even, w_out_even, diff_qk_gain, diff_lambda,
           diff_subln_gain, nsa_qk_gain, nsa_cmp_pe, nsa_cmp_w1, nsa_cmp_w2, w_in_odd, w_out_odd,
           hgrn_norm_gain, hgrn_lb_logits):
    b, t, d = x_prompt.shape
    s, ts, _ = x_sample.shape
    depth = norm_mix.shape[0]
    n_even = cache_diff_kv.shape[0]
    n_pool, page_size = cache_diff_kv.shape[1], cache_diff_kv.shape[2]
    past_len = page_table.shape[1] * page_size
    n_buf = cache_nsa_win.shape[2]
    tsp = SAMPLE_ROWS
    assert ts <= tsp and d == D_MODEL

    lbw = jax.nn.softmax(hgrn_lb_logits.astype(F32), axis=0)
    lower_bounds = jnp.cumsum(lbw, axis=0) - lbw[0]

    hp = x_prompt.reshape(b * t, d)
    hs = _pad_rows(x_sample, tsp).reshape(s * tsp, d)
    tm_p = _row_tile(b * t, 512)
    tm_s = _row_tile(s * tsp, 256)
    cache_diff4 = cache_diff_kv.reshape(n_even, n_pool, page_size * 2 * DA_HEADS, DA_VDIM)
    cache_nsa_t = jnp.transpose(cache_nsa_kv, (0, 1, 3, 4, 5, 2)).reshape(
        n_even, n_pool, 4 * NSA_GROUPS * HEAD_DIM, page_size)
    cache_win = cache_nsa_win.reshape(n_even, s, n_buf, 2 * NSA_GROUPS * HEAD_DIM)
    live = (jnp.arange(tsp) < ts)[None, :, None]

    dkv_p, dkv_s, nkv_p, nkv_s, win_p, win_s, hg_p, hg_s = [], [], [], [], [], [], [], []
    for l in range(depth):
        if l % 2 == 0:
            e = l // 2
            lam_init = 0.8 - 0.6 * math.exp(-0.3 * l)
            w_pad, colgain, cw, cw_t = _even_weights(w_in_even[e], diff_qk_gain[e], nsa_qk_gain[e],
                                               nsa_cmp_pe[e], nsa_cmp_w1[e], nsa_cmp_w2[e])
            w_out = w_out_even[e].astype(BF16)
            pf, pb = _norm_proj(hp, norm_mix[l], w_pad, colgain, EVEN_NORM_BLOCKS, tm_p, True)
            pf3, pb3 = pf.reshape(b, t, -1), pb.reshape(b, t, -1)
            feat_major = lambda c0, c1: jnp.swapaxes(pb3[:, :, c0:c1], 1, 2)
            da = _diff_prompt(pb3, feat_major(C_DV, C_NQ), diff_lambda[e], diff_subln_gain[e], lam_init,
                              tq=_row_tile(t, 512), tk=_row_tile(t, 512))
            kcb, vcb = _compress_prompt(pf3, cw)
            onsa = _nsa_prompt_t(pf3, pb3, feat_major(C_NKV + 3 * LANES, C_NKV + 4 * LANES),
                                 feat_major(C_NKV + 5 * LANES, C_NKV + 6 * LANES), kcb, vcb,
                                 tq=_row_tile(t, 256), tk=_row_tile(t, 256))
            hp = _out_proj(da.reshape(b * t, -1), 0, onsa.reshape(b * t, -1), 0, hp, w_out, tm_p)
            dkv_p.append(pf3[:, :, C_DK:C_NQ].reshape(b, t, 2, DA_HEADS, DA_VDIM))
            nkv_p.append(pf3[:, :, C_NKV:C_NKV + 512].reshape(b, t, 4, NSA_GROUPS, HEAD_DIM))
            nw = min(NSA_WINDOW, t)
            win_p.append(pf3[:, t - nw:, C_NKV + 512:C_NG].reshape(b, nw, 2, NSA_GROUPS, HEAD_DIM))
            sf, sb = _norm_proj(hs, norm_mix[l], w_pad, colgain, EVEN_NORM_BLOCKS, tm_s, True)
            sf3, sb3 = sf.reshape(s, tsp, -1), sb.reshape(s, tsp, -1)
            diff_tail = _pad_rows(sf3[:, :, C_DK:C_NQ], LANES)
            da_s = _diff_sample(sb3, diff_tail, cache_diff4, e, page_table, diff_lambda[e],
                                diff_subln_gain[e], lam_init, past_len)
            as_page = lambda x: jnp.swapaxes(_pad_rows(x, page_size), 1, 2)
            cmp_tail = as_page(jnp.where(live, sf3[:, :, C_NKV:C_NKV + 256], 0.0))
            kcb_s, vcb_s = _compress_sample(cmp_tail, cache_nsa_t, e, page_table, cw_t)
            sel_tail = as_page(sf3[:, :, C_NKV + 256:C_NKV + 512])
            new_win = sf3[:, :, C_NKV + 512:C_NG]
            kwv = jnp.concatenate([cache_win[e], _pad_rows(new_win, page_size)], axis=1)
            onsa_s = _nsa_sample_t(sf3, sb3, kcb_s, vcb_s, sel_tail, kwv, cache_nsa_t, e, page_table,
                                   past_len, ts)
            hs = _out_proj(da_s.reshape(s * tsp, -1), 0, onsa_s.reshape(s * tsp, -1), 0, hs, w_out, tm_s)
            dkv_s.append(sf3[:, :ts, C_DK:C_NQ].reshape(s, ts, 2, DA_HEADS, DA_VDIM))
            nkv_s.append(sf3[:, :ts, C_NKV:C_NKV + 512].reshape(s, ts, 4, NSA_GROUPS, HEAD_DIM))
            win_all = jnp.concatenate([cache_win[e], new_win[:, :ts]], axis=1)[:, -n_buf:]
            win_s.append(win_all.reshape(s, n_buf, 2, NSA_GROUPS, HEAD_DIM))
        else:
            r = l // 2
            w_in = w_in_odd[r].astype(BF16)
            w_out = w_out_odd[r].astype(BF16)
            ones = jnp.ones((1, ODD_IN), F32)
            pf, = _norm_proj(hp, norm_mix[l], w_in, ones, (), tm_p, False)
            chunk = math.gcd(t, HG_CHUNK)
            o_p, st_p = _hgrn(pf.reshape(b, t, -1), lower_bounds[l], hgrn_norm_gain[r],
                              jnp.zeros((b, HG_HEADS, HG_DK, HG_DK), F32), chunk, _row_tile(t, 512), None)
            hp = _out_proj(o_p.reshape(b * t, -1), 0, o_p.reshape(b * t, -1), 1, hp, w_out, tm_p)
            hg_p.append(jnp.swapaxes(st_p, -1, -2))
            sf, = _norm_proj(hs, norm_mix[l], w_in, ones, (), tm_s, False)
            o_s, st_s = _hgrn(sf.reshape(s, tsp, -1), lower_bounds[l], hgrn_norm_gain[r],
                              jnp.swapaxes(state_hgrn[r].astype(F32), -1, -2), tsp, tsp, ts)
            hs = _out_proj(o_s.reshape(s * tsp, -1), 0, o_s.reshape(s * tsp, -1), 1, hs, w_out, tm_s)
            hg_s.append(jnp.swapaxes(st_s, -1, -2))
        w_fi = w_ffn_in[l].astype(BF16)
        w_fo = w_ffn_out[l].astype(BF16)
        tf = w_fo.shape[0] // 2
        hp = _ffn(hp, norm_ffn[l], w_fi, w_fo, tm_p, tf)
        hs = _ffn(hs, norm_ffn[l], w_fi, w_fo, tm_s, tf)

    y_s = hs.reshape(s, tsp, d)[:, :ts]
    return (hp.reshape(b, t, d), y_s, jnp.stack(dkv_p), jnp.stack(dkv_s), jnp.stack(nkv_p), jnp.stack(nkv_s),
            jnp.stack(win_p), jnp.stack(win_s), jnp.stack(hg_p), jnp.stack(hg_s))
```
